```python
import math
import jax, jax.numpy as jnp
from jax import lax
import numpy as np

D_MODEL = 4096
BATCH = 4
SEQ = 2048
DEPTH = 4
DEC_BATCH = 8
DEC_SEQ = 4
PAST_LEN = 8192
PAGE_SIZE = 128

H_SB = 8
DH_SB = 128
W_SB = H_SB * DH_SB
SB_QBLOCK = 128
SB_BIAS_INIT = -6.0
C_CONV = D_MODEL // 4
CONV_W = 31
H_ML = 8
DK_ML = D_MODEL // 32
DV_ML = D_MODEL // 16
W_ML_QK = H_ML * DK_ML
W_ML_V = H_ML * DV_ML
ML_CHUNK = 64
N_BRANCH = 3
FF = -(-8 * D_MODEL // (3 * 256)) * 256
EPS = 1e-6
IN_SIZES = (W_SB, W_SB, W_SB, 2 * C_CONV, W_ML_QK, W_ML_QK, W_ML_V, W_ML_V, H_ML, H_ML, N_BRANCH * D_MODEL)
N_IN = 3 * W_SB + 2 * C_CONV + 2 * W_ML_QK + 2 * W_ML_V + 2 * H_ML + N_BRANCH * D_MODEL

kernel_name = "hybrid_sb_conformer_mlstm_decoder_step"


def _rms_norm(x, g):
    xf = x.astype(jnp.float32)
    y = xf * lax.rsqrt(jnp.mean(xf * xf, axis=-1, keepdims=True) + EPS)
    return (y * g.astype(jnp.float32)).astype(x.dtype)


def _layer_norm(x, g, b):
    xf = x.astype(jnp.float32)
    mu = jnp.mean(xf, axis=-1, keepdims=True)
    var = jnp.mean(jnp.square(xf - mu), axis=-1, keepdims=True)
    y = (xf - mu) * lax.rsqrt(var + EPS)
    return (y * g.astype(jnp.float32) + b.astype(jnp.float32)).astype(x.dtype)


def _split_cols(proj):
    pts = []
    acc = 0
    for s in IN_SIZES[:-1]:
        acc += s
        pts.append(acc)
    return jnp.split(proj, pts, axis=-1)


def _stick_breaking(q, k, v, q_pos, k_pos, bias):
    b, tq, h, dh = q.shape
    blk = SB_QBLOCK if tq % SB_QBLOCK == 0 else tq
    nb = tq // blk
    qb = jnp.moveaxis(q.reshape(b, nb, blk, h, dh), 1, 0)
    pb = q_pos.reshape(nb, blk)
    scale = 1.0 / math.sqrt(dh)
    bias_f = bias.astype(jnp.float32)[None, :, None, None]

    def block(args):
        qi, pi = args
        z = jnp.einsum("bqhd,bkhd->bhqk", qi, k, preferred_element_type=jnp.float32) * scale + bias_f
        mask = k_pos[None, :] < pi[:, None]
        log_keep = jnp.where(mask, jax.nn.log_sigmoid(-z), 0.0)
        after = lax.cumsum(log_keep, axis=3, reverse=True) - log_keep
        w = jnp.where(mask, jnp.exp(jax.nn.log_sigmoid(z) + after), 0.0)
        return jnp.einsum("bhqk,bkhd->bqhd", w.astype(v.dtype), v)

    out = lax.map(block, (qb, pb))
    return jnp.moveaxis(out, 0, 1).reshape(b, tq, h, dh)


def _causal_depthwise(u_cat, w, bias):
    c = u_cat.shape[-1]
    out = lax.conv_general_dilated(u_cat, w[:, None, :].astype(u_cat.dtype), window_strides=(1,),
                                   padding="VALID", dimension_numbers=("NWC", "WIO", "NWC"),
                                   feature_group_count=c)
    return out + bias.astype(u_cat.dtype)


def _mlstm(q, k, v, ig, lf, c0, n0, m0):
    b, t, h, dk = q.shape
    dv = v.shape[-1]
    L = ML_CHUNK if t % ML_CHUNK == 0 else t
    nc = t // L

    def chunks(a):
        a = a.astype(jnp.float32).reshape((b, nc, L) + a.shape[2:])
        return jnp.moveaxis(jnp.moveaxis(a, 1, 0), 2, 3)

    causal = jnp.tril(jnp.ones((L, L), dtype=bool))

    def step(carry, xs):
        c, n, m = carry
        qi, ki, vi, ii, fi = xs
        bcum = jnp.cumsum(fi, axis=-1)
        log_d = jnp.where(causal, bcum[..., :, None] - bcum[..., None, :] + ii[..., None, :], -jnp.inf)
        log_inter = bcum + m[..., None]
        m_t = jnp.maximum(log_inter, jnp.max(log_d, axis=-1))
        s = jnp.einsum("bhld,bhsd->bhls", qi, ki) * jnp.exp(log_d - m_t[..., None])
        e_inter = jnp.exp(log_inter - m_t)
        num = e_inter[..., None] * jnp.einsum("bhld,bhde->bhle", qi, c) + jnp.einsum("bhls,bhse->bhle", s, vi)
        den = e_inter * jnp.einsum("bhld,bhd->bhl", qi, n) + jnp.sum(s, axis=-1)
        h_out = num / jnp.maximum(jnp.abs(den), jnp.exp(-m_t))[..., None]
        b_last = bcum[..., -1]
        log_w = b_last[..., None] - bcum + ii
        m_new = jnp.maximum(b_last + m, jnp.max(log_w, axis=-1))
        w = jnp.exp(log_w - m_new[..., None])
        decay = jnp.exp(b_last + m - m_new)
        c_new = decay[..., None, None] * c + jnp.einsum("bhs,bhsd,bhse->bhde", w, ki, vi)
        n_new = decay[..., None] * n + jnp.einsum("bhs,bhsd->bhd", w, ki)
        return (c_new, n_new, m_new), h_out

    carry0 = (c0.astype(jnp.float32), n0.astype(jnp.float32), m0.astype(jnp.float32))
    (c, n, m), hs = lax.scan(step, carry0, (chunks(q), chunks(k), chunks(v), chunks(ig), chunks(lf)))
    hs = jnp.transpose(hs, (1, 0, 3, 2, 4)).reshape(b, t, h, dv)
    return hs, c, n, m


def _layer(x, cvec, p, q_pos, past):
    bsz, t, _ = x.shape
    mod = cvec @ p["w_ada"] + p["b_ada"]
    sh1, sc1, gt1, sh2, sc2, gt2 = jnp.split(mod[:, None, :], 6, axis=-1)
    h = _rms_norm(x, p["g_norm1"]) * (1.0 + sc1) + sh1
    qa, ka, va, glu, qm, km, vm, om, im, fm, gates = _split_cols(h @ p["w_in"])

    qa = _rms_norm(qa.reshape(bsz, t, H_SB, DH_SB), p["g_q"])
    ka = _rms_norm(ka.reshape(bsz, t, H_SB, DH_SB), p["g_k"])
    va = va.reshape(bsz, t, H_SB, DH_SB)
    if past is None:
        keys, vals, k_pos = ka, va, q_pos
    else:
        keys = jnp.concatenate([past["k"].astype(ka.dtype), ka], axis=1)
        vals = jnp.concatenate([past["v"].astype(va.dtype), va], axis=1)
        k_pos = jnp.concatenate([past["pos"], q_pos])
    ya = _stick_breaking(qa, keys, vals, q_pos, k_pos, p["b_sb"]).reshape(bsz, t, W_SB)

    a, g = jnp.split(glu, 2, axis=-1)
    u = a * jax.nn.sigmoid(g)
    if past is None:
        hist = jnp.zeros((bsz, CONV_W - 1, C_CONV), u.dtype)
    else:
        hist = past["conv"].astype(u.dtype)
    u_cat = jnp.concatenate([hist, u], axis=1)
    yb = jax.nn.silu(_layer_norm(_causal_depthwise(u_cat, p["w_dw"], p["b_dw"]), p["g_ln"], p["b_ln"]))
    new_conv = u_cat[:, u_cat.shape[1] - (CONV_W - 1):]

    qm = qm.reshape(bsz, t, H_ML, DK_ML)
    km = km.reshape(bsz, t, H_ML, DK_ML) * (1.0 / math.sqrt(DK_ML))
    vm = vm.reshape(bsz, t, H_ML, DV_ML)
    ig = (im + p["b_ig"]).astype(jnp.float32)
    lf = jax.nn.log_sigmoid((fm + p["b_fg"]).astype(jnp.float32))
    if past is None:
        c0 = jnp.zeros((bsz, H_ML, DK_ML, DV_ML), jnp.float32)
        n0 = jnp.zeros((bsz, H_ML, DK_ML), jnp.float32)
        m0 = jnp.zeros((bsz, H_ML), jnp.float32)
    else:
        c0, n0, m0 = past["C"], past["n"], past["m"]
    hm, c_new, n_new, m_new = _mlstm(qm, km, vm, ig, lf, c0, n0, m0)
    yc = _rms_norm(hm.astype(x.dtype), p["g_hn"]).reshape(bsz, t, W_ML_V) * jax.nn.sigmoid(om)

    g3 = jax.nn.sigmoid(gates.reshape(bsz, t, N_BRANCH, D_MODEL))
    merged = (g3[:, :, 0] * (ya @ p["w_pa"]) + g3[:, :, 1] * (yb @ p["w_pb"])
              + g3[:, :, 2] * (yc @ p["w_pc"]))
    x = x + gt1 * (merged @ p["w_out"])

    h2 = _rms_norm(x, p["g_norm2"]) * (1.0 + sc2) + sh2
    x = x + gt2 * ((jax.nn.silu(h2 @ p["w_f1"]) * (h2 @ p["w_f3"])) @ p["w_f2"])
    return x, (ka, va, new_conv, c_new, n_new, m_new)


def setup_inputs(seed: int = 0) -> dict:
    key = jax.random.key(seed)
    ks = jax.random.split(key, 33)
    n_pages = PAST_LEN // PAGE_SIZE
    n_used = DEC_BATCH * n_pages
    n_pool = n_used + max(1, n_used // 4)

    def nrm(k, shape, s):
        return s * jax.random.normal(k, shape, jnp.float32)

    page_table = jax.random.permutation(ks[8], n_pool)[:n_used].reshape(DEC_BATCH, n_pages).astype(jnp.int32)
    dinv = D_MODEL ** -0.5
    return {
        "x_prompt": nrm(ks[0], (BATCH, SEQ, D_MODEL), 1.0),
        "x_sample": nrm(ks[1], (DEC_BATCH, DEC_SEQ, D_MODEL), 1.0),
        "cache_k": nrm(ks[2], (DEPTH, n_pool, PAGE_SIZE, H_SB, DH_SB), 1.0),
        "cache_v": nrm(ks[3], (DEPTH, n_pool, PAGE_SIZE, H_SB, DH_SB), 1.0),
        "state_conv": nrm(ks[4], (DEPTH, DEC_BATCH, CONV_W - 1, C_CONV), 0.5),
        "state_C": nrm(ks[5], (DEPTH, DEC_BATCH, H_ML, DK_ML, DV_ML), 0.1),
        "state_n": nrm(ks[6], (DEPTH, DEC_BATCH, H_ML, DK_ML), 0.1),
        "state_m": nrm(ks[7], (DEPTH, DEC_BATCH, H_ML), 0.5),
        "page_table": page_table,
        "c_prompt": nrm(ks[9], (BATCH, D_MODEL), 1.0),
        "c_sample": nrm(ks[10], (DEC_BATCH, D_MODEL), 1.0),
        "w_ada": nrm(ks[11], (DEPTH, D_MODEL, 6 * D_MODEL), 0.5 * dinv),
        "b_ada": nrm(ks[12], (DEPTH, 6 * D_MODEL), 0.02),
        "g_norm1": 1.0 + nrm(ks[13], (DEPTH, D_MODEL), 0.02),
        "g_norm2": 1.0 + nrm(ks[14], (DEPTH, D_MODEL), 0.02),
        "w_in": nrm(ks[15], (DEPTH, D_MODEL, N_IN), dinv),
        "g_q": 1.0 + nrm(ks[16], (DEPTH, DH_SB), 0.02),
        "g_k": 1.0 + nrm(ks[17], (DEPTH, DH_SB), 0.02),
        "b_sb": SB_BIAS_INIT + nrm(ks[32], (DEPTH, H_SB), 0.1),
        "w_dw": nrm(ks[18], (DEPTH, CONV_W, C_CONV), CONV_W ** -0.5),
        "b_dw": nrm(ks[19], (DEPTH, C_CONV), 0.02),
        "g_ln": 1.0 + nrm(ks[20], (DEPTH, C_CONV), 0.02),
        "b_ln": nrm(ks[21], (DEPTH, C_CONV), 0.02),
        "b_ig": -2.0 + nrm(ks[22], (DEPTH, H_ML), 0.1),
        "b_fg": 3.0 + nrm(ks[23], (DEPTH, H_ML), 0.1),
        "g_hnorm": 1.0 + nrm(ks[24], (DEPTH, H_ML, DV_ML), 0.02),
        "w_pa": nrm(ks[25], (DEPTH, W_SB, D_MODEL), W_SB ** -0.5),
        "w_pb": nrm(ks[26], (DEPTH, C_CONV, D_MODEL), C_CONV ** -0.5),
        "w_pc": nrm(ks[27], (DEPTH, W_ML_V, D_MODEL), W_ML_V ** -0.5),
        "w_out": nrm(ks[28], (DEPTH, D_MODEL, D_MODEL), dinv),
        "w_ffn1": nrm(ks[29], (DEPTH, D_MODEL, FF), dinv),
        "w_ffn3": nrm(ks[30], (DEPTH, D_MODEL, FF), dinv),
        "w_ffn2": nrm(ks[31], (DEPTH, FF, D_MODEL), FF ** -0.5),
    }


def reference(x_prompt, x_sample, cache_k, cache_v, state_conv, state_C, state_n, state_m, page_table,
              c_prompt, c_sample, w_ada, b_ada, g_norm1, g_norm2, w_in, g_q, g_k, b_sb, w_dw, b_dw, g_ln, b_ln,
              b_ig, b_fg, g_hnorm, w_pa, w_pb, w_pc, w_out, w_ffn1, w_ffn3, w_ffn2):
    dec_b = x_sample.shape[0]
    past_len = page_table.shape[1] * cache_k.shape[2]
    pos_prompt = jnp.arange(x_prompt.shape[1], dtype=jnp.int32)
    pos_past = jnp.arange(past_len, dtype=jnp.int32)
    pos_sample = past_len + jnp.arange(x_sample.shape[1], dtype=jnp.int32)
    yp, ys = x_prompt, x_sample
    outs_p = [[] for _ in range(6)]
    outs_s = [[] for _ in range(6)]
    for l in range(DEPTH):
        p = {"w_ada": w_ada[l], "b_ada": b_ada[l], "g_norm1": g_norm1[l], "g_norm2": g_norm2[l],
             "w_in": w_in[l], "g_q": g_q[l], "g_k": g_k[l], "b_sb": b_sb[l], "w_dw": w_dw[l], "b_dw": b_dw[l],
             "g_ln": g_ln[l], "b_ln": b_ln[l], "b_ig": b_ig[l], "b_fg": b_fg[l], "g_hn": g_hnorm[l],
             "w_pa": w_pa[l], "w_pb": w_pb[l], "w_pc": w_pc[l], "w_out": w_out[l],
             "w_f1": w_ffn1[l], "w_f3": w_ffn3[l], "w_f2": w_ffn2[l]}
        yp, st_p = _layer(yp, c_prompt, p, pos_prompt, None)
        past = {"k": cache_k[l][page_table].reshape(dec_b, past_len, H_SB, DH_SB),
                "v": cache_v[l][page_table].reshape(dec_b, past_len, H_SB, DH_SB),
                "pos": pos_past, "conv": state_conv[l], "C": state_C[l], "n": state_n[l], "m": state_m[l]}
        ys, st_s = _layer(ys, c_sample, p, pos_sample, past)
        for i in range(6):
            outs_p[i].append(st_p[i])
            outs_s[i].append(st_s[i])
    k_p, v_p, conv_p, C_p, n_p, m_p = [jnp.stack(o) for o in outs_p]
    k_s, v_s, conv_s, C_s, n_s, m_s = [jnp.stack(o) for o in outs_s]
    return (yp, ys, k_p, v_p, conv_p, C_p, n_p, m_p, k_s, v_s, conv_s, C_s, n_s, m_s)
```

```python
import functools
import math

import jax
import jax.numpy as jnp
from jax import lax
from jax.experimental import pallas as pl
from jax.experimental.pallas import tpu as pltpu

BF = jnp.bfloat16
F32 = jnp.float32
EPS = 1e-6
NEG = -1e30
LANES = 128
CB = 1024
SAMPLE_T = 8
VMEM_LIMIT = 60 * 1024 * 1024


def _cp(*sem):
    return pltpu.CompilerParams(dimension_semantics=sem, vmem_limit_bytes=VMEM_LIMIT)


def _resident(shape, index_map):
    return pl.BlockSpec(shape, index_map, pipeline_mode=pl.Buffered(1))


def _mod_spec(l, which, rows_per_group, tm, tn=None):
    bpg = rows_per_group // tm

    def full(i, *_):
        return (l, which, i // bpg, 0, 0)

    def tiled(i, j):
        return (l, which, i // bpg, 0, j)

    return full if tn is None else tiled


def _wdot(a, w_ref, kc=None):
    k = w_ref.shape[0]
    kc = k if kc is None else kc
    acc = None
    for k0 in range(0, k, kc):
        w = w_ref[k0:k0 + kc, :]
        if w.dtype != BF:
            w = w.astype(BF)
        p = jnp.dot(a[:, k0:k0 + kc], w, preferred_element_type=F32)
        acc = p if acc is None else acc + p
    return acc


def _ada_kernel(c_ref, w_ref, b_ref, o_ref):
    o_ref[...] = _wdot(c_ref, w_ref) + b_ref[...]


def _adaln(c_all, w_ada, b_ada):
    nl, d, n = w_ada.shape
    r = c_all.shape[0]
    tn = 512
    return pl.pallas_call(
        _ada_kernel,
        grid=(nl, n // tn),
        in_specs=[pl.BlockSpec((r, d), lambda l, j: (0, 0)),
                  pl.BlockSpec((None, d, tn), lambda l, j: (l, 0, j)),
                  pl.BlockSpec((None, 1, tn), lambda l, j: (l, 0, j))],
        out_specs=pl.BlockSpec((None, r, tn), lambda l, j: (l, 0, j)),
        out_shape=jax.ShapeDtypeStruct((nl, r, n), F32),
        compiler_params=_cp("arbitrary", "arbitrary"),
        name="adaln",
    )(c_all, w_ada, b_ada.reshape(nl, 1, n))


def _norm_kernel(x_ref, g_ref, sc_ref, sh_ref, o_ref):
    x = x_ref[...]
    y = x * lax.rsqrt(jnp.mean(x * x, axis=-1, keepdims=True) + EPS) * g_ref[...]
    o_ref[...] = (y * (1.0 + sc_ref[...]) + sh_ref[...]).astype(o_ref.dtype)


def _norm_mod(x, g, mod5, l, which_sc, which_sh, rpg):
    m, d = x.shape
    tm = min(256, m)
    r = mod5.shape[3]
    rb = tm if r > 1 else 1
    return pl.pallas_call(
        _norm_kernel,
        grid=(m // tm,),
        in_specs=[pl.BlockSpec((tm, d), lambda i: (i, 0)),
                  pl.BlockSpec((None, 1, d), lambda i: (l, 0, 0)),
                  pl.BlockSpec((None, None, None, rb, d), _mod_spec(l, which_sc, rpg, tm)),
                  pl.BlockSpec((None, None, None, rb, d), _mod_spec(l, which_sh, rpg, tm))],
        out_specs=pl.BlockSpec((tm, d), lambda i: (i, 0)),
        out_shape=jax.ShapeDtypeStruct((m, d), BF),
        compiler_params=_cp("arbitrary"),
        name="norm_mod",
    )(x, g.reshape(g.shape[0], 1, d), mod5, mod5)


def _mm_kernel(a_ref, w_ref, o_ref):
    o_ref[...] = _wdot(a_ref, w_ref).astype(o_ref.dtype)


def _mm(a, w, l, n_cols, tm, tn, out_dtype=F32):
    m, k = a.shape
    stacked = w.ndim == 3
    wspec = (pl.BlockSpec((None, k, tn), lambda i, j: (l, 0, j)) if stacked
             else pl.BlockSpec((k, tn), lambda i, j: (0, j)))
    return pl.pallas_call(
        _mm_kernel,
        grid=(m // tm, n_cols // tn),
        in_specs=[_resident((tm, k), lambda i, j: (i, 0)), wspec],
        out_specs=pl.BlockSpec((tm, tn), lambda i, j: (i, j)),
        out_shape=jax.ShapeDtypeStruct((m, n_cols), out_dtype),
        compiler_params=_cp("arbitrary", "arbitrary"),
        name="mm",
    )(a, w)


def _mm_res_kernel(a_ref, w_ref, x_ref, gt_ref, o_ref, *, kc):
    o_ref[...] = x_ref[...] + gt_ref[...] * _wdot(a_ref, w_ref, kc)


def _mm_residual(a, w, l, x, mod5, which_gt, rpg, tm, tn, kc=None):
    m, k = a.shape
    n = w.shape[2]
    r = mod5.shape[3]
    rb = tm if r > 1 else 1
    return pl.pallas_call(
        functools.partial(_mm_res_kernel, kc=kc),
        grid=(m // tm, n // tn),
        in_specs=[_resident((tm, k), lambda i, j: (i, 0)),
                  pl.BlockSpec((None, k, tn), lambda i, j: (l, 0, j)),
                  pl.BlockSpec((tm, tn), lambda i, j: (i, j)),
                  pl.BlockSpec((None, None, None, rb, tn), _mod_spec(l, which_gt, rpg, tm, tn))],
        out_specs=pl.BlockSpec((tm, tn), lambda i, j: (i, j)),
        out_shape=jax.ShapeDtypeStruct((m, n), F32),
        compiler_params=_cp("arbitrary", "arbitrary"),
        name="mm_residual",
    )(a, w, x, mod5)


def _merge_kernel(h_ref, ya_ref, yb_ref, yc_ref, g0_ref, g1_ref, g2_ref, wa_ref, wb_ref, wc_ref, o_ref):
    h = h_ref[...]
    out = jax.nn.sigmoid(_wdot(h, g0_ref)) * _wdot(ya_ref, wa_ref)
    out = out + jax.nn.sigmoid(_wdot(h, g1_ref)) * _wdot(yb_ref, wb_ref)
    out = out + jax.nn.sigmoid(_wdot(h, g2_ref)) * _wdot(yc_ref, wc_ref)
    o_ref[...] = out.astype(o_ref.dtype)


def _merge(h, ya, yb, yc, w_gates, w_pa, w_pb, w_pc, l, tm, tn):
    m, d = h.shape
    nb = d // tn

    def wspec(k, off):
        return pl.BlockSpec((None, k, tn), lambda i, j: (l, 0, j + off))

    return pl.pallas_call(
        _merge_kernel,
        grid=(m // tm, nb),
        in_specs=[_resident((tm, d), lambda i, j: (i, 0)),
                  _resident((tm, ya.shape[1]), lambda i, j: (i, 0)),
                  _resident((tm, yb.shape[1]), lambda i, j: (i, 0)),
                  _resident((tm, yc.shape[1]), lambda i, j: (i, 0)),
                  wspec(d, 0), wspec(d, nb), wspec(d, 2 * nb),
                  wspec(ya.shape[1], 0), wspec(yb.shape[1], 0), wspec(yc.shape[1], 0)],
        out_specs=pl.BlockSpec((tm, tn), lambda i, j: (i, j)),
        out_shape=jax.ShapeDtypeStruct((m, d), BF),
        compiler_params=_cp("arbitrary", "arbitrary"),
        name="merge",
    )(h, ya, yb, yc, w_gates, w_gates, w_gates, w_pa, w_pb, w_pc)


def _ffn13_kernel(a_ref, w1_ref, w3_ref, o_ref):
    a = a_ref[...]
    o_ref[...] = (jax.nn.silu(_wdot(a, w1_ref)) * _wdot(a, w3_ref)).astype(o_ref.dtype)


def _ffn13(a, w1, w3, l, tm, tn):
    m, k = a.shape
    n = w1.shape[2]
    return pl.pallas_call(
        _ffn13_kernel,
        grid=(m // tm, n // tn),
        in_specs=[_resident((tm, k), lambda i, j: (i, 0)),
                  pl.BlockSpec((None, k, tn), lambda i, j: (l, 0, j)),
                  pl.BlockSpec((None, k, tn), lambda i, j: (l, 0, j))],
        out_specs=pl.BlockSpec((tm, tn), lambda i, j: (i, j)),
        out_shape=jax.ShapeDtypeStruct((m, n), BF),
        compiler_params=_cp("arbitrary", "arbitrary"),
        name="ffn13",
    )(a, w1, w3)


def _qkv_kernel(q_ref, k_ref, v_ref, gq_ref, gk_ref, qb_ref, kf_ref, kb_ref, vf_ref, vb_ref, *, nh, dh):
    for h in range(nh):
        sl = slice(h * dh, (h + 1) * dh)
        q = q_ref[:, sl]
        k = k_ref[:, sl]
        qn = q * lax.rsqrt(jnp.mean(q * q, axis=-1, keepdims=True) + EPS) * gq_ref[...]
        kn = k * lax.rsqrt(jnp.mean(k * k, axis=-1, keepdims=True) + EPS) * gk_ref[...]
        qb_ref[:, sl] = qn.astype(BF)
        kf_ref[:, sl] = kn
        kb_ref[:, sl] = kn.astype(BF)
    v = v_ref[...]
    vf_ref[...] = v
    vb_ref[...] = v.astype(BF)


def _qkv_prep(proj, g_q, g_k, l, w_sb):
    m = proj.shape[0]
    dh = g_q.shape[1]
    tm = min(256, m)
    cb = lambda c: pl.BlockSpec((tm, w_sb), lambda i: (i, c))
    gspec = pl.BlockSpec((None, 1, dh), lambda i: (l, 0, 0))
    ob = pl.BlockSpec((tm, w_sb), lambda i: (i, 0))
    sd = lambda dt: jax.ShapeDtypeStruct((m, w_sb), dt)
    return pl.pallas_call(
        functools.partial(_qkv_kernel, nh=w_sb // dh, dh=dh),
        grid=(m // tm,),
        in_specs=[cb(0), cb(1), cb(2), gspec, gspec],
        out_specs=[ob, ob, ob, ob, ob],
        out_shape=[sd(BF), sd(F32), sd(BF), sd(F32), sd(BF)],
        compiler_params=_cp("arbitrary"),
        name="qkv_prep",
    )(proj, proj, proj, g_q.reshape(-1, 1, dh), g_k.reshape(-1, 1, dh))


def _split3(x):
    x1 = x.astype(BF)
    r1 = x - x1.astype(F32)
    x2 = r1.astype(BF)
    x3 = (r1 - x2.astype(F32)).astype(BF)
    return x1, x2, x3


def _softplus(z):
    return jnp.maximum(z, 0.0) + jnp.log1p(jnp.exp(-jnp.abs(z)))


def _sb_block(z, valid, vb, run, tri):
    r, c = z.shape
    cw = tri.shape[0]
    nchunk = c // cw
    lk = jnp.where(valid, -_softplus(z), 0.0)
    chunks = [lk[:, i * cw:(i + 1) * cw] for i in range(nchunk)]
    st = chunks[0] if nchunk == 1 else jnp.concatenate(chunks, axis=0)
    hi = st.astype(BF)
    lo = (st - hi.astype(F32)).astype(BF)
    after_st = (jnp.dot(hi, tri, preferred_element_type=F32)
                + jnp.dot(lo, tri, preferred_element_type=F32))
    tots = [jnp.sum(ch, axis=-1, keepdims=True) for ch in chunks]
    suffix = run
    parts = [None] * nchunk
    for i in reversed(range(nchunk)):
        parts[i] = after_st[i * r:(i + 1) * r] + suffix
        suffix = suffix + tots[i]
    after = parts[0] if nchunk == 1 else jnp.concatenate(parts, axis=1)
    w = jnp.where(valid, jnp.exp(z + lk + after), 0.0)
    out = jnp.dot(w.astype(BF), vb, preferred_element_type=F32)
    return out, suffix


def _tri(n):
    row = lax.broadcasted_iota(jnp.int32, (n, n), 0)
    col = lax.broadcasted_iota(jnp.int32, (n, n), 1)
    return (row > col).astype(BF)


def _sb_prompt_kernel(q_ref, k_ref, v_ref, bias_ref, o_ref, *, tq, scale):
    i = pl.program_id(2)
    q = q_ref[...]
    bias = bias_ref[...]
    tri = _tri(tq)
    row = lax.broadcasted_iota(jnp.int32, (tq, tq), 0)
    col = lax.broadcasted_iota(jnp.int32, (tq, tq), 1)
    strict = col < row

    def body(step, carry):
        acc, run = carry
        j0 = pl.multiple_of((i - step) * tq, tq)
        k = k_ref[pl.ds(j0, tq), :]
        v = v_ref[pl.ds(j0, tq), :]
        z = lax.dot_general(q, k, (((1,), (1,)), ((), ())), preferred_element_type=F32) * scale + bias
        valid = jnp.logical_or(strict, step > 0)
        out, run = _sb_block(z, valid, v, run, tri)
        return acc + out, run

    acc0 = jnp.zeros((tq, q.shape[1]), F32)
    run0 = jnp.zeros((tq, 1), F32)
    acc, _ = lax.fori_loop(0, i + 1, body, (acc0, run0))
    o_ref[...] = acc.astype(o_ref.dtype)


def _sb_prompt(qb, kb, vb, b_sb, l, bsz, t, dh):
    m, w = qb.shape
    nh = w // dh
    tq = min(256, t)
    nq = t // tq
    return pl.pallas_call(
        functools.partial(_sb_prompt_kernel, tq=tq, scale=1.0 / math.sqrt(dh)),
        grid=(bsz, nh, nq),
        in_specs=[pl.BlockSpec((tq, dh), lambda b, h, i: (b * nq + i, h)),
                  pl.BlockSpec((t, dh), lambda b, h, i: (b, h)),
                  pl.BlockSpec((t, dh), lambda b, h, i: (b, h)),
                  pl.BlockSpec((None, None, 1, 1), lambda b, h, i: (l, h, 0, 0))],
        out_specs=pl.BlockSpec((tq, dh), lambda b, h, i: (b * nq + i, h)),
        out_shape=jax.ShapeDtypeStruct((m, w), BF),
        compiler_params=_cp("arbitrary", "arbitrary", "arbitrary"),
        name="sb_prompt",
    )(qb, kb, vb, b_sb.reshape(b_sb.shape[0], nh, 1, 1))


def _sb_sample_kernel(pt_ref, q_ref, kn_ref, vn_ref, bias_ref, *refs, pps, nh, tq, scale):
    k_refs = refs[:pps]
    v_refs = refs[pps:2 * pps]
    o_ref = refs[2 * pps]
    acc_ref, run_ref = refs[2 * pps + 1:]
    s = pl.program_id(1)
    q = q_ref[...]
    bias = bias_ref[...]
    r = q.shape[0]
    nt = (((1,), (1,)), ((), ()))

    @pl.when(s == 0)
    def _():
        n = kn_ref.shape[0]
        rr = lax.broadcasted_iota(jnp.int32, (r, n), 0)
        cc = lax.broadcasted_iota(jnp.int32, (r, n), 1)
        valid = jnp.logical_and(rr // tq == cc % nh, cc // nh < rr % tq)
        z = lax.dot_general(q, kn_ref[...].astype(BF), nt, preferred_element_type=F32) * scale + bias
        out, run = _sb_block(z, valid, vn_ref[...].astype(BF), jnp.zeros((r, 1), F32), _tri(n))
        acc_ref[...] = out
        run_ref[...] = run

    c = k_refs[0].shape[0]
    rr = lax.broadcasted_iota(jnp.int32, (r, c), 0)
    cc = lax.broadcasted_iota(jnp.int32, (r, c), 1)
    valid = rr // tq == cc % nh
    tri = _tri(2 * LANES)
    acc = acc_ref[...]
    run = run_ref[...]
    for p in range(pps):
        z = lax.dot_general(q, k_refs[p][...].astype(BF), nt, preferred_element_type=F32) * scale + bias
        out, run = _sb_block(z, valid, v_refs[p][...].astype(BF), run, tri)
        acc = acc + out
    acc_ref[...] = acc
    run_ref[...] = run

    @pl.when(s == pl.num_programs(1) - 1)
    def _():
        o_ref[...] = acc.astype(o_ref.dtype)


def _sb_sample(q_hq, k_new, v_new, bias_rows, cache_k, cache_v, page_table, l, nh, tq):
    bsz, r, dh = q_hq.shape
    n_pages = page_table.shape[1]
    pps = math.gcd(8, n_pages)
    prow = cache_k.shape[2]
    nn = k_new.shape[1]

    def page_spec(p):
        return pl.BlockSpec((None, None, prow, dh),
                            lambda b, s, pt: (l, pt[b, n_pages - 1 - (s * pps + p)], 0, 0))

    grid_spec = pltpu.PrefetchScalarGridSpec(
        num_scalar_prefetch=1,
        grid=(bsz, n_pages // pps),
        in_specs=[pl.BlockSpec((None, r, dh), lambda b, s, pt: (b, 0, 0)),
                  pl.BlockSpec((None, nn, dh), lambda b, s, pt: (b, 0, 0)),
                  pl.BlockSpec((None, nn, dh), lambda b, s, pt: (b, 0, 0)),
                  pl.BlockSpec((r, 1), lambda b, s, pt: (0, 0))]
                 + [page_spec(p) for p in range(pps)] * 2,
        out_specs=pl.BlockSpec((None, r, dh), lambda b, s, pt: (b, 0, 0)),
        scratch_shapes=[pltpu.VMEM((r, dh), F32), pltpu.VMEM((r, 1), F32)],
    )
    return pl.pallas_call(
        functools.partial(_sb_sample_kernel, pps=pps, nh=nh, tq=tq, scale=1.0 / math.sqrt(dh)),
        grid_spec=grid_spec,
        out_shape=jax.ShapeDtypeStruct((bsz, r, dh), BF),
        compiler_params=_cp("arbitrary", "arbitrary"),
        name="sb_sample",
    )(page_table, q_hq, k_new, v_new, bias_rows, *([cache_k] * pps), *([cache_v] * pps))


CONV_PAD = 32
CONV_ROWS = 64


def _conv_kernel(*refs, tt, tv, cw, has_hist):
    if has_hist:
        a_ref, g_ref, hist_ref, w_ref, bdw_ref, gln_ref, bln_ref, y_ref, st_ref, s_ref, c_ref = refs
    else:
        a_ref, g_ref, w_ref, bdw_ref, gln_ref, bln_ref, y_ref, st_ref, s_ref, c_ref = refs
    i = pl.program_id(1)
    hrows = cw - 1
    nch = a_ref.shape[1]

    @pl.when(i == 0)
    def _():
        s_ref[0:CONV_PAD, :] = hist_ref[...] if has_hist else jnp.zeros((CONV_PAD, nch), F32)

    s_ref[CONV_PAD:CONV_PAD + tt, :] = a_ref[...] * jax.nn.sigmoid(g_ref[...])

    rs = min(CONV_ROWS, tt)

    def chan(c, carry):
        c0 = pl.multiple_of(c * LANES, LANES)
        lanes = pl.ds(c0, LANES)
        for r0 in range(0, tt, rs):
            acc = jnp.broadcast_to(bdw_ref[:, lanes], (rs, LANES))
            for j in range(cw):
                acc = acc + w_ref[j:j + 1, lanes] * s_ref[pl.ds(CONV_PAD - hrows + r0 + j, rs), lanes]
            c_ref[r0:r0 + rs, lanes] = acc
        return carry

    lax.fori_loop(0, nch // LANES, chan, 0)

    x = c_ref[...]
    mu = jnp.mean(x, axis=-1, keepdims=True)
    xc = x - mu
    var = jnp.mean(xc * xc, axis=-1, keepdims=True)
    y = xc * lax.rsqrt(var + EPS) * gln_ref[...] + bln_ref[...]
    y_ref[...] = (y * jax.nn.sigmoid(y)).astype(y_ref.dtype)

    @pl.when(i == pl.num_programs(1) - 1)
    def _():
        st_ref[...] = s_ref[pl.ds(CONV_PAD + tv - hrows, hrows), :]

    s_ref[0:CONV_PAD, :] = s_ref[tt:tt + CONV_PAD, :]


def _conv_module(proj3, cb_a, hist, w_dw, b_dw, g_ln, b_ln, l, t_valid):
    bsz, t, _ = proj3.shape
    cw, nch = w_dw.shape[1:]
    tt = min(256, t)
    nblk = t // tt
    tv = tt if t_valid == t else t_valid
    assert t_valid == t or nblk == 1
    vec = lambda a: a.reshape(a.shape[0], 1, nch)
    vspec = pl.BlockSpec((None, 1, nch), lambda b, i: (l, 0, 0))
    in_specs = [pl.BlockSpec((None, tt, nch), lambda b, i: (b, i, cb_a)),
                pl.BlockSpec((None, tt, nch), lambda b, i: (b, i, cb_a + 1))]
    args = [proj3, proj3]
    if hist is not None:
        in_specs.append(pl.BlockSpec((None, CONV_PAD, nch), lambda b, i: (b, 0, 0)))
        args.append(hist)
    in_specs += [pl.BlockSpec((None, cw, nch), lambda b, i: (l, 0, 0)), vspec, vspec, vspec]
    args += [w_dw, vec(b_dw), vec(g_ln), vec(b_ln)]
    return pl.pallas_call(
        functools.partial(_conv_kernel, tt=tt, tv=tv, cw=cw, has_hist=hist is not None),
        grid=(bsz, nblk),
        in_specs=in_specs,
        out_specs=[pl.BlockSpec((None, tt, nch), lambda b, i: (b, i, 0)),
                   pl.BlockSpec((None, cw - 1, nch), lambda b, i: (b, 0, 0))],
        out_shape=[jax.ShapeDtypeStruct((bsz, t, nch), BF),
                   jax.ShapeDtypeStruct((bsz, cw - 1, nch), F32)],
        scratch_shapes=[pltpu.VMEM((max(CONV_PAD, tt) + CONV_PAD, nch), F32), pltpu.VMEM((tt, nch), F32)],
        compiler_params=_cp("arbitrary", "arbitrary"),
        name="conv_module",
    )(*args)


def _dot_split(x, m01, left):
    acc = None
    for part in _split3(x):
        p = (jnp.dot(m01, part, preferred_element_type=F32) if left
             else jnp.dot(part, m01, preferred_element_type=F32))
        acc = p if acc is None else acc + p
    return acc


def _mlstm_kernel(*refs, nh, dk, dv, tv, has_state, kscale):
    if has_state:
        (q_ref, k_ref, v0_ref, v1_ref, o0_ref, o1_ref, gc_ref, gr_ref, bc_ref, br_ref, ghn_ref,
         c0_ref, n0_ref, m0_ref, y_ref, cout_ref, nout_ref, mout_ref, c_s, n_s, m_s) = refs
    else:
        (q_ref, k_ref, v0_ref, v1_ref, o0_ref, o1_ref, gc_ref, gr_ref, bc_ref, br_ref, ghn_ref,
         y_ref, cout_ref, nout_ref, mout_ref, c_s, n_s, m_s) = refs
    ci = pl.program_id(1)
    ln = q_ref.shape[0]

    @pl.when(ci == 0)
    def _():
        if has_state:
            c_s[...] = c0_ref[...]
            n_s[...] = n0_ref[...]
            m_s[...] = m0_ref[...]
        else:
            c_s[...] = jnp.zeros_like(c_s)
            n_s[...] = jnp.zeros_like(n_s)
            m_s[...] = jnp.zeros_like(m_s)

    pre_c = gc_ref[...] + bc_ref[...]
    lf_c = jax.nn.log_sigmoid(pre_c)
    pre_r = gr_ref[...] + br_ref[...]
    lf_r = jax.nn.log_sigmoid(pre_r)
    if tv < ln:
        keep_c = lax.broadcasted_iota(jnp.int32, pre_c.shape, 0) < tv
        keep_r = lax.broadcasted_iota(jnp.int32, pre_r.shape, 1) < tv
        pre_c = jnp.where(keep_c, pre_c, NEG)
        pre_r = jnp.where(keep_r, pre_r, NEG)
        lf_c = jnp.where(keep_c, lf_c, 0.0)
        lf_r = jnp.where(keep_r, lf_r, 0.0)
    row = lax.broadcasted_iota(jnp.int32, (ln, ln), 0)
    col = lax.broadcasted_iota(jnp.int32, (ln, ln), 1)
    causal = col <= row
    lower = causal.astype(BF)
    upper = (row <= col).astype(BF)
    bcum_c = _dot_split(lf_c, lower, left=True)
    bcum_r = _dot_split(lf_r, upper, left=False)

    hpb = v0_ref.shape[1] // dv
    for h in range(nh):
        v_ref, o_ref = (v0_ref, o0_ref) if h < hpb else (v1_ref, o1_ref)
        vs = slice((h % hpb) * dv, (h % hpb + 1) * dv)
        ks = slice(h * dk, (h + 1) * dk)
        bc = bcum_c[:, nh + h:nh + h + 1]
        ic = pre_c[:, h:h + 1]
        brow = bcum_r[nh + h:nh + h + 1, :]
        irow = pre_r[h:h + 1, :]
        m = m_s[:, h:h + 1]
        cst = c_s[h]
        nst = n_s[h:h + 1, :]

        qf = q_ref[:, ks]
        qb = qf.astype(BF)
        kf = k_ref[:, ks] * kscale
        kb = kf.astype(BF)
        vb = v_ref[:, vs].astype(BF)

        log_d = jnp.where(causal, bc - brow + irow, NEG)
        log_inter = bc + m
        m_t = jnp.maximum(log_inter, jnp.max(log_d, axis=-1, keepdims=True))
        s = (lax.dot_general(qb, kb, (((1,), (1,)), ((), ())), preferred_element_type=F32)
             * jnp.exp(log_d - m_t))
        e_inter = jnp.exp(log_inter - m_t)
        num = (e_inter * jnp.dot(qb, cst.astype(BF), preferred_element_type=F32)
               + jnp.dot(s.astype(BF), vb, preferred_element_type=F32))
        den = (e_inter * jnp.sum(qf * nst, axis=-1, keepdims=True)
               + jnp.sum(s, axis=-1, keepdims=True))
        hm = num / jnp.maximum(jnp.abs(den), jnp.exp(-m_t))
        y = hm * lax.rsqrt(jnp.mean(hm * hm, axis=-1, keepdims=True) + EPS) * ghn_ref[h:h + 1, :]
        y_ref[:, h * dv:(h + 1) * dv] = (y * jax.nn.sigmoid(o_ref[:, vs])).astype(y_ref.dtype)

        b_last = bc[ln - 1:ln, :]
        m_new = jnp.maximum(b_last + m, jnp.max(b_last - brow + irow, axis=-1, keepdims=True))
        w_c = jnp.exp(b_last - bc + ic - m_new)
        decay = jnp.exp(b_last + m - m_new)
        wk = w_c * kf
        c_s[h] = decay * cst + lax.dot_general(wk.astype(BF), vb, (((0,), (0,)), ((), ())),
                                               preferred_element_type=F32)
        n_s[h:h + 1, :] = decay * nst + jnp.sum(wk, axis=0, keepdims=True)
        m_s[:, h:h + 1] = m_new

    @pl.when(ci == pl.num_programs(1) - 1)
    def _():
        cout_ref[...] = c_s[...]
        nout_ref[...] = n_s[...]
        mout_ref[...] = m_s[:, 0:nh]


def _mlstm(proj3, cb_q, gates_c, gates_r, bias_c, bias_r, g_hn, state, l, nh, dk, dv, t_valid):
    bsz, t, _ = proj3.shape
    ln = min(256, t)
    nc = t // ln
    tv = ln if t_valid == t else t_valid
    assert t_valid == t or nc == 1
    assert nh * dk == CB and nh * dv == 2 * CB
    blk = lambda c: pl.BlockSpec((None, ln, CB), lambda b, i: (b, i, cb_q + c))
    in_specs = [blk(0), blk(1), blk(2), blk(3), blk(4), blk(5),
                pl.BlockSpec((None, ln, LANES), lambda b, i: (b, i, 0)),
                pl.BlockSpec((None, 2 * nh, ln), lambda b, i: (b, 0, i)),
                pl.BlockSpec((None, 1, LANES), lambda b, i: (l, 0, 0)),
                pl.BlockSpec((None, 2 * nh, 1), lambda b, i: (l, 0, 0)),
                pl.BlockSpec((None, nh, dv), lambda b, i: (l, 0, 0))]
    args = [proj3] * 6 + [gates_c, gates_r, bias_c, bias_r, g_hn]
    st_specs = [pl.BlockSpec((None, nh, dk, dv), lambda b, i: (b, 0, 0, 0)),
                pl.BlockSpec((None, nh, dk), lambda b, i: (b, 0, 0)),
                pl.BlockSpec((None, 1, nh), lambda b, i: (b, 0, 0))]
    if state is not None:
        in_specs += st_specs
        args += list(state)
    return pl.pallas_call(
        functools.partial(_mlstm_kernel, nh=nh, dk=dk, dv=dv, tv=tv, has_state=state is not None,
                          kscale=1.0 / math.sqrt(dk)),
        grid=(bsz, nc),
        in_specs=in_specs,
        out_specs=[pl.BlockSpec((None, ln, nh * dv), lambda b, i: (b, i, 0))] + st_specs,
        out_shape=[jax.ShapeDtypeStruct((bsz, t, nh * dv), BF),
                   jax.ShapeDtypeStruct((bsz, nh, dk, dv), F32),
                   jax.ShapeDtypeStruct((bsz, nh, dk), F32),
                   jax.ShapeDtypeStruct((bsz, 1, nh), F32)],
        scratch_shapes=[pltpu.VMEM((nh, dk, dv), F32), pltpu.VMEM((nh, dk), F32), pltpu.VMEM((1, nh), F32)],
        compiler_params=_cp("arbitrary", "arbitrary"),
        name="mlstm",
    )(*args)


def _layer(x, mod5, rpg, p, l, dims, past, tm_big):
    bsz, t, d = x.shape
    m = bsz * t
    tv = dims["t_valid"]
    w_sb, nh_sb, dh = dims["w_sb"], dims["nh_sb"], dims["dh"]
    nh, dk, dv = dims["nh_ml"], dims["dk"], dims["dv"]
    x2 = x.reshape(m, d)
    tm = min(tm_big, rpg)
    tm_half = min(tm_big // 2, rpg)

    h = _norm_mod(x2, p["g_norm1"], mod5, l, 1, 0, rpg)
    proj = _mm(h, p["w_in"], l, dims["n_a"], tm, 512)
    gates_c = _mm(h, p["w_if"], l, LANES, tm, LANES)
    proj3 = proj.reshape(bsz, t, -1)

    qb, kf, kb, vf, vb = _qkv_prep(proj, p["g_q"], p["g_k"], l, w_sb)
    if past is None:
        ya = _sb_prompt(qb, kb, vb, p["b_sb"], l, bsz, t, dh)
        k_out, v_out = kf.reshape(bsz, t, nh_sb, dh), vf.reshape(bsz, t, nh_sb, dh)
    else:
        q_hq = qb.reshape(bsz, t, nh_sb, dh)[:, :tv].transpose(0, 2, 1, 3).reshape(bsz, nh_sb * tv, dh)
        k_new = kf.reshape(bsz, t, nh_sb, dh)[:, :tv]
        v_new = vf.reshape(bsz, t, nh_sb, dh)[:, :tv]
        bias_rows = jnp.repeat(p["b_sb"][l], tv).reshape(nh_sb * tv, 1)
        o = _sb_sample(q_hq, k_new.reshape(bsz, tv * nh_sb, dh), v_new.reshape(bsz, tv * nh_sb, dh), bias_rows,
                       past["k"], past["v"], past["page_table"], l, nh_sb, tv)
        o = o.reshape(bsz, nh_sb, tv, dh).transpose(0, 2, 1, 3).reshape(bsz, tv, w_sb)
        ya = jnp.pad(o, ((0, 0), (0, t - tv), (0, 0))).reshape(m, w_sb)
        k_out, v_out = k_new, v_new

    hist = None if past is None else past["conv"]
    yb, conv_out = _conv_module(proj3, dims["cb_glu"], hist, p["w_dw"], p["b_dw"], p["g_ln"], p["b_ln"], l, tv)

    gates3 = gates_c.reshape(bsz, t, LANES)
    gates_r = gates3[:, :, :2 * nh].transpose(0, 2, 1)
    state = None if past is None else (past["C"], past["n"], past["m"])
    yc, c_out, n_out, m_out = _mlstm(proj3, dims["cb_qm"], gates3, gates_r, p["gate_bias_c"], p["gate_bias_r"],
                                     p["g_hn"], state, l, nh, dk, dv, tv)

    merged = _merge(h, ya, yb.reshape(m, -1), yc.reshape(m, -1), p["w_gates"], p["w_pa"], p["w_pb"], p["w_pc"],
                    l, tm_half, 256)
    x2 = _mm_residual(merged, p["w_out"], l, x2, mod5, 2, rpg, tm, 256)

    h2 = _norm_mod(x2, p["g_norm2"], mod5, l, 4, 3, rpg)
    u = _ffn13(h2, p["w_f1"], p["w_f3"], l, tm, 256)
    x2 = _mm_residual(u, p["w_f2"], l, x2, mod5, 5, rpg, tm_half, 256, kc=dims["ffn_kc"])
    return x2.reshape(bsz, t, d), (k_out, v_out, conv_out, c_out, n_out, m_out.reshape(bsz, nh))


def kernel(x_prompt, x_sample, cache_k, cache_v, state_conv, state_C, state_n, state_m, page_table,
           c_prompt, c_sample, w_ada, b_ada, g_norm1, g_norm2, w_in, g_q, g_k, b_sb, w_dw, b_dw, g_ln, b_ln,
           b_ig, b_fg, g_hnorm, w_pa, w_pb, w_pc, w_out, w_ffn1, w_ffn3, w_ffn2):
    depth, d, n_in = w_in.shape
    bsz, t, _ = x_prompt.shape
    dbs, dt, _ = x_sample.shape
    w_sb = w_pa.shape[1]
    dh = g_q.shape[1]
    nch = w_pb.shape[1]
    nh_ml, dv = g_hnorm.shape[1:]
    w_qk = (n_in - 3 * w_sb - 2 * nch - 2 * nh_ml * dv - 2 * nh_ml - 3 * d) // 2
    off_glu = 3 * w_sb
    off_qm = off_glu + 2 * nch
    off_if = off_qm + 2 * w_qk + 2 * nh_ml * dv
    off_gates = off_if + 2 * nh_ml
    assert w_sb == CB and nch == CB and w_qk == CB and off_if % 512 == 0
    ff = w_ffn1.shape[2]
    kc = next((ff // c for c in range(2, 17) if ff % (c * LANES) == 0), ff)
    dims = dict(w_sb=w_sb, nh_sb=w_sb // dh, dh=dh, nh_ml=nh_ml, dk=w_qk // nh_ml, dv=dv, n_a=off_if,
                cb_glu=off_glu // CB, cb_qm=off_qm // CB, ffn_kc=kc)

    nc = bsz + dbs
    rpad = -(-nc // 16) * 16
    c_all = jnp.pad(jnp.concatenate([c_prompt, c_sample], axis=0), ((0, rpad - nc), (0, 0))).astype(BF)
    mod = _adaln(c_all, w_ada, b_ada).reshape(depth, rpad, 6, d)
    mod_p = mod[:, :bsz].transpose(0, 2, 1, 3).reshape(depth, 6, bsz, 1, d)
    mod_s = jnp.repeat(mod[:, bsz:nc], SAMPLE_T, axis=1).transpose(0, 2, 1, 3).reshape(depth, 6, 1, dbs * SAMPLE_T, d)

    w_if = jnp.pad(w_in[:, :, off_if:off_gates], ((0, 0), (0, 0), (0, LANES - 2 * nh_ml)))
    w_gates = w_in[:, :, off_gates:].astype(BF)
    gate_bias = jnp.concatenate([b_ig, b_fg], axis=1)
    params = dict(g_norm1=g_norm1, g_norm2=g_norm2, w_in=w_in, w_if=w_if, w_gates=w_gates, g_q=g_q, g_k=g_k,
                  b_sb=b_sb, w_dw=w_dw, b_dw=b_dw, g_ln=g_ln, b_ln=b_ln,
                  gate_bias_c=jnp.pad(gate_bias, ((0, 0), (0, LANES - 2 * nh_ml))).reshape(depth, 1, LANES),
                  gate_bias_r=gate_bias.reshape(depth, 2 * nh_ml, 1),
                  g_hn=g_hnorm, w_pa=w_pa, w_pb=w_pb, w_pc=w_pc, w_out=w_out,
                  w_f1=w_ffn1, w_f3=w_ffn3, w_f2=w_ffn2)

    pool, page, nh_sb = cache_k.shape[1], cache_k.shape[2], cache_k.shape[3]
    ck = cache_k.reshape(depth, pool, page * nh_sb, dh)
    cv = cache_v.reshape(depth, pool, page * nh_sb, dh)
    hist = jnp.pad(state_conv, ((0, 0), (0, 0), (CONV_PAD - state_conv.shape[2], 0), (0, 0)))

    yp = x_prompt
    ys = jnp.pad(x_sample, ((0, 0), (0, SAMPLE_T - dt), (0, 0)))
    outs_p = [[] for _ in range(6)]
    outs_s = [[] for _ in range(6)]
    for l in range(depth):
        yp, st_p = _layer(yp, mod_p, t, params, l, dict(dims, t_valid=t), None, 2048)
        past = dict(k=ck, v=cv, page_table=page_table, conv=hist[l], C=state_C[l], n=state_n[l],
                    m=state_m[l].reshape(dbs, 1, nh_ml))
        ys, st_s = _layer(ys, mod_s, dbs * SAMPLE_T, params, l, dict(dims, t_valid=dt), past, 2048)
        for i in range(6):
            outs_p[i].append(st_p[i])
            outs_s[i].append(st_s[i])
    outs_p = [jnp.stack(o) for o in outs_p]
    outs_s = [jnp.stack(o) for o in outs_s]
    return (yp, ys[:, :dt], *outs_p, *outs_s)
```

```python
import functools
import math

import jax
import jax.numpy as jnp
from jax import lax
from jax.experimental import pallas as pl
from jax.experimental.pallas import tpu as pltpu

BF = jnp.bfloat16
F32 = jnp.float32
EPS = 1e-6
NEG = -1e30
LANES = 128
SUBLANES = 8
CB = 1024
SAMPLE_T = 8
VMEM_LIMIT = 60 * 1024 * 1024


def _cp(*sem):
    return pltpu.CompilerParams(dimension_semantics=sem, vmem_limit_bytes=VMEM_LIMIT)


def _resident(shape, index_map):
    return pl.BlockSpec(shape, index_map, pipeline_mode=pl.Buffered(1))


def _mod_spec(l, which, rows_per_group, tm, tn=None):
    bpg = rows_per_group // tm

    def full(i, *_):
        return (l, which, i // bpg, 0, 0)

    def tiled(i, j):
        return (l, which, i // bpg, 0, j)

    return full if tn is None else tiled


def _wdot(a, w_ref, kc=None):
    k = w_ref.shape[0]
    kc = k if kc is None else kc
    acc = None
    for k0 in range(0, k, kc):
        w = w_ref[k0:k0 + kc, :]
        if w.dtype != BF:
            w = w.astype(BF)
        p = jnp.dot(a[:, k0:k0 + kc], w, preferred_element_type=F32)
        acc = p if acc is None else acc + p
    return acc


def _wdot_t(a, wt_ref):
    return lax.dot_general(a[...], wt_ref[...].astype(BF), (((1,), (1,)), ((), ())), preferred_element_type=F32)


def _ada_kernel(c_ref, w_ref, b_ref, o_ref):
    o_ref[...] = _wdot(c_ref, w_ref) + b_ref[...]


def _adaln(c_all, w_ada, b_ada):
    nl, d, n = w_ada.shape
    r = c_all.shape[0]
    tn = 512
    return pl.pallas_call(
        _ada_kernel,
        grid=(nl, n // tn),
        in_specs=[pl.BlockSpec((r, d), lambda l, j: (0, 0)),
                  pl.BlockSpec((None, d, tn), lambda l, j: (l, 0, j)),
                  pl.BlockSpec((None, 1, tn), lambda l, j: (l, 0, j))],
        out_specs=pl.BlockSpec((None, r, tn), lambda l, j: (l, 0, j)),
        out_shape=jax.ShapeDtypeStruct((nl, r, n), F32),
        compiler_params=_cp("arbitrary", "arbitrary"),
        name="adaln",
    )(c_all, w_ada, b_ada.reshape(nl, 1, n))


def _norm_kernel(x_ref, g_ref, sc_ref, sh_ref, o_ref):
    x = x_ref[...]
    y = x * lax.rsqrt(jnp.mean(x * x, axis=-1, keepdims=True) + EPS) * g_ref[...]
    o_ref[...] = (y * (1.0 + sc_ref[...]) + sh_ref[...]).astype(o_ref.dtype)


def _norm_mod(x, g, mod5, l, which_sc, which_sh, rpg):
    m, d = x.shape
    tm = min(256, m)
    r = mod5.shape[3]
    rb = tm if r > 1 else 1
    return pl.pallas_call(
        _norm_kernel,
        grid=(m // tm,),
        in_specs=[pl.BlockSpec((tm, d), lambda i: (i, 0)),
                  pl.BlockSpec((None, 1, d), lambda i: (l, 0, 0)),
                  pl.BlockSpec((None, None, None, rb, d), _mod_spec(l, which_sc, rpg, tm)),
                  pl.BlockSpec((None, None, None, rb, d), _mod_spec(l, which_sh, rpg, tm))],
        out_specs=pl.BlockSpec((tm, d), lambda i: (i, 0)),
        out_shape=jax.ShapeDtypeStruct((m, d), BF),
        compiler_params=_cp("arbitrary"),
        name="norm_mod",
    )(x, g.reshape(g.shape[0], 1, d), mod5, mod5)


def _sample_tile(os_ref, fn):
    first = pl.program_id(0) == 0

    @pl.when(first)
    def _():
        os_ref[...] = fn().astype(os_ref.dtype)

    @pl.when(jnp.logical_not(first))
    def _():
        os_ref[...] = jnp.zeros_like(os_ref)


def _sample_out(nblk, ms, n, dtype, tn):
    return (pl.BlockSpec((None, ms, tn), lambda i, j: (i, 0, j)), jax.ShapeDtypeStruct((nblk, ms, n), dtype))


def _mm_kernel(a_ref, wt_ref, as_ref, o_ref, os_ref):
    o_ref[...] = _wdot_t(a_ref, wt_ref).astype(o_ref.dtype)
    _sample_tile(os_ref, lambda: _wdot_t(as_ref, wt_ref))


def _mm(a, a_s, w_t, l, cb0, n_cols, tm, tn):
    m, k = a.shape
    ms = a_s.shape[0]
    os_spec, os_shape = _sample_out(m // tm, ms, n_cols, F32, tn)
    o, o_s = pl.pallas_call(
        _mm_kernel,
        grid=(m // tm, n_cols // tn),
        in_specs=[_resident((tm, k), lambda i, j: (i, 0)),
                  pl.BlockSpec((None, tn, k), lambda i, j: (l, j + cb0, 0)),
                  _resident((ms, k), lambda i, j: (0, 0))],
        out_specs=[pl.BlockSpec((tm, tn), lambda i, j: (i, j)), os_spec],
        out_shape=[jax.ShapeDtypeStruct((m, n_cols), F32), os_shape],
        compiler_params=_cp("arbitrary", "arbitrary"),
        name="mm",
    )(a, w_t, a_s)
    return o, o_s[0]


def _mm_res_kernel(a_ref, w_ref, x_ref, gt_ref, as_ref, xs_ref, gts_ref, o_ref, os_ref, *, kc):
    o_ref[...] = x_ref[...] + gt_ref[...] * _wdot(a_ref, w_ref, kc)
    _sample_tile(os_ref, lambda: xs_ref[...] + gts_ref[...] * _wdot(as_ref, w_ref, kc))


def _mm_residual(a, a_s, w, l, x, x_s, mod_p, mod_s, which_gt, rpg, tm, tn, kc=None):
    m, k = a.shape
    ms = a_s.shape[0]
    n = w.shape[2]
    os_spec, os_shape = _sample_out(m // tm, ms, n, F32, tn)
    o, o_s = pl.pallas_call(
        functools.partial(_mm_res_kernel, kc=kc),
        grid=(m // tm, n // tn),
        in_specs=[_resident((tm, k), lambda i, j: (i, 0)),
                  pl.BlockSpec((None, k, tn), lambda i, j: (l, 0, j)),
                  pl.BlockSpec((tm, tn), lambda i, j: (i, j)),
                  pl.BlockSpec((None, None, None, 1, tn), _mod_spec(l, which_gt, rpg, tm, tn)),
                  _resident((ms, k), lambda i, j: (0, 0)),
                  pl.BlockSpec((ms, tn), lambda i, j: (0, j)),
                  pl.BlockSpec((None, None, None, ms, tn), lambda i, j: (l, which_gt, 0, 0, j))],
        out_specs=[pl.BlockSpec((tm, tn), lambda i, j: (i, j)), os_spec],
        out_shape=[jax.ShapeDtypeStruct((m, n), F32), os_shape],
        compiler_params=_cp("arbitrary", "arbitrary"),
        name="mm_residual",
    )(a, w, x, mod_p, a_s, x_s, mod_s)
    return o, o_s[0]


def _merge_kernel(h_ref, ya_ref, yb_ref, yc_ref, g0_ref, g1_ref, g2_ref, wa_ref, wb_ref, wc_ref,
                  hs_ref, yas_ref, ybs_ref, ycs_ref, o_ref, os_ref):
    def merged(h_r, ya_r, yb_r, yc_r):
        h = h_r[...]
        out = jax.nn.sigmoid(_wdot_t(h, g0_ref)) * _wdot(ya_r, wa_ref)
        out = out + jax.nn.sigmoid(_wdot_t(h, g1_ref)) * _wdot(yb_r, wb_ref)
        return out + jax.nn.sigmoid(_wdot_t(h, g2_ref)) * _wdot(yc_r, wc_ref)

    o_ref[...] = merged(h_ref, ya_ref, yb_ref, yc_ref).astype(o_ref.dtype)
    _sample_tile(os_ref, lambda: merged(hs_ref, yas_ref, ybs_ref, ycs_ref))


def _cast_rows_kernel(x_ref, o_ref):
    o_ref[...] = x_ref[0].astype(o_ref.dtype)


def _cast_rows(w_t, row0, nrows):
    nl, _, k = w_t.shape
    tr = 512
    assert row0 % 16 == 0 and nrows % tr == 0
    return pl.pallas_call(
        _cast_rows_kernel,
        grid=(nl, nrows // tr),
        in_specs=[pl.BlockSpec((pl.Element(1), pl.Element(tr), pl.Element(k)),
                               lambda l, r: (l, pl.multiple_of(row0 + r * tr, 16), 0))],
        out_specs=pl.BlockSpec((None, tr, k), lambda l, r: (l, r, 0)),
        out_shape=jax.ShapeDtypeStruct((nl, nrows, k), BF),
        compiler_params=_cp("arbitrary", "arbitrary"),
        name="cast_rows",
    )(w_t)


def _merge(acts, acts_s, wg_t, w_pa, w_pb, w_pc, l, tm, tn):
    m, d = acts[0].shape
    ms = acts_s[0].shape[0]
    nb = d // tn

    def gspec(b):
        return pl.BlockSpec((None, tn, d), lambda i, j: (l, b * nb + j, 0))

    def wspec(k):
        return pl.BlockSpec((None, k, tn), lambda i, j: (l, 0, j))

    os_spec, os_shape = _sample_out(m // tm, ms, d, BF, tn)
    o, o_s = pl.pallas_call(
        _merge_kernel,
        grid=(m // tm, d // tn),
        in_specs=[_resident((tm, a.shape[1]), lambda i, j: (i, 0)) for a in acts]
                 + [gspec(0), gspec(1), gspec(2)]
                 + [wspec(a.shape[1]) for a in acts[1:]]
                 + [_resident((ms, a.shape[1]), lambda i, j: (0, 0)) for a in acts_s],
        out_specs=[pl.BlockSpec((tm, tn), lambda i, j: (i, j)), os_spec],
        out_shape=[jax.ShapeDtypeStruct((m, d), BF), os_shape],
        compiler_params=_cp("arbitrary", "arbitrary"),
        name="merge",
    )(*acts, wg_t, wg_t, wg_t, w_pa, w_pb, w_pc, *acts_s)
    return o, o_s[0]


def _ffn13_kernel(a_ref, w1_ref, w3_ref, as_ref, o_ref, os_ref):
    def swiglu(a_r):
        a = a_r[...]
        return jax.nn.silu(_wdot(a, w1_ref)) * _wdot(a, w3_ref)

    o_ref[...] = swiglu(a_ref).astype(o_ref.dtype)
    _sample_tile(os_ref, lambda: swiglu(as_ref))


def _ffn13(a, a_s, w1, w3, l, tm, tn):
    m, k = a.shape
    ms = a_s.shape[0]
    n = w1.shape[2]
    os_spec, os_shape = _sample_out(m // tm, ms, n, BF, tn)
    o, o_s = pl.pallas_call(
        _ffn13_kernel,
        grid=(m // tm, n // tn),
        in_specs=[_resident((tm, k), lambda i, j: (i, 0)),
                  pl.BlockSpec((None, k, tn), lambda i, j: (l, 0, j)),
                  pl.BlockSpec((None, k, tn), lambda i, j: (l, 0, j)),
                  _resident((ms, k), lambda i, j: (0, 0))],
        out_specs=[pl.BlockSpec((tm, tn), lambda i, j: (i, j)), os_spec],
        out_shape=[jax.ShapeDtypeStruct((m, n), BF), os_shape],
        compiler_params=_cp("arbitrary", "arbitrary"),
        name="ffn13",
    )(a, w1, w3, a_s)
    return o, o_s[0]


def _qkv_kernel(q_ref, k_ref, v_ref, gq_ref, gk_ref, qb_ref, kf_ref, kb_ref, vf_ref, vb_ref, *, nh, dh):
    for h in range(nh):
        sl = slice(h * dh, (h + 1) * dh)
        q = q_ref[:, sl]
        k = k_ref[:, sl]
        qn = q * lax.rsqrt(jnp.mean(q * q, axis=-1, keepdims=True) + EPS) * gq_ref[...]
        kn = k * lax.rsqrt(jnp.mean(k * k, axis=-1, keepdims=True) + EPS) * gk_ref[...]
        qb_ref[:, sl] = qn.astype(BF)
        kf_ref[:, sl] = kn
        kb_ref[:, sl] = kn.astype(BF)
    v = v_ref[...]
    vf_ref[...] = v
    vb_ref[...] = v.astype(BF)


def _qkv_prep(proj, g_q, g_k, l, w_sb):
    m = proj.shape[0]
    dh = g_q.shape[1]
    tm = min(256, m)
    cb = lambda c: pl.BlockSpec((tm, w_sb), lambda i: (i, c))
    gspec = pl.BlockSpec((None, 1, dh), lambda i: (l, 0, 0))
    ob = pl.BlockSpec((tm, w_sb), lambda i: (i, 0))
    sd = lambda dt: jax.ShapeDtypeStruct((m, w_sb), dt)
    return pl.pallas_call(
        functools.partial(_qkv_kernel, nh=w_sb // dh, dh=dh),
        grid=(m // tm,),
        in_specs=[cb(0), cb(1), cb(2), gspec, gspec],
        out_specs=[ob, ob, ob, ob, ob],
        out_shape=[sd(BF), sd(F32), sd(BF), sd(F32), sd(BF)],
        compiler_params=_cp("arbitrary"),
        name="qkv_prep",
    )(proj, proj, proj, g_q.reshape(-1, 1, dh), g_k.reshape(-1, 1, dh))


def _split3(x):
    x1 = x.astype(BF)
    r1 = x - x1.astype(F32)
    x2 = r1.astype(BF)
    x3 = (r1 - x2.astype(F32)).astype(BF)
    return x1, x2, x3


def _softplus(z):
    return jnp.maximum(z, 0.0) + jnp.log(1.0 + jnp.exp(-jnp.abs(z)))


def _sb_blocks(zs, valid, vbs, runs, tri, chain=False):
    r, c = zs[0].shape
    cw = tri.shape[0]
    nchunk = c // cw
    cat = lambda parts, axis: parts[0] if len(parts) == 1 else jnp.concatenate(parts, axis=axis)
    chunks = lambda x: [x[:, i * cw:(i + 1) * cw] for i in range(nchunk)]

    lks, after_sts, tots = [], [], []
    for z in zs:
        lk = -_softplus(z)
        if valid is not None:
            lk = jnp.where(valid, lk, 0.0)
        st = cat(chunks(lk), 0)
        hi = st.astype(BF)
        lo = (st - hi.astype(F32)).astype(BF)
        lks.append(lk)
        tots.append([jnp.sum(ch, axis=-1, keepdims=True) for ch in chunks(lk)])
        after_sts.append(jnp.dot(hi, tri, preferred_element_type=F32)
                         + jnp.dot(lo, tri, preferred_element_type=F32))
    ws, new_runs = [], []
    for n, (z, lk, after_st, tot) in enumerate(zip(zs, lks, after_sts, tots)):
        suffix = new_runs[-1] if (chain and n > 0) else runs[n]
        parts = [None] * nchunk
        for i in reversed(range(nchunk)):
            parts[i] = after_st[i * r:(i + 1) * r] + suffix
            suffix = suffix + tot[i]
        w = jnp.exp(z + lk + cat(parts, 1))
        if valid is not None:
            w = jnp.where(valid, w, 0.0)
        ws.append(w.astype(BF))
        new_runs.append(suffix)
    outs = [jnp.dot(w, vb, preferred_element_type=F32) for w, vb in zip(ws, vbs)]
    return outs, new_runs


def _sb_block(z, valid, vb, run, tri):
    outs, runs = _sb_blocks([z], valid, [vb], [run], tri)
    return outs[0], runs[0]


def _tri(n):
    row = lax.broadcasted_iota(jnp.int32, (n, n), 0)
    col = lax.broadcasted_iota(jnp.int32, (n, n), 1)
    return (row > col).astype(BF)


SB_HEADS = 8


def _sb_prompt_kernel(q_ref, k_ref, v_ref, bias_ref, o_ref, *, tq, dh, scale):
    i = pl.program_id(2)
    hp = q_ref.shape[1] // dh
    tri = _tri(tq)
    row = lax.broadcasted_iota(jnp.int32, (tq, tq), 0)
    col = lax.broadcasted_iota(jnp.int32, (tq, tq), 1)
    strict = col < row
    nt = (((1,), (1,)), ((), ()))

    def block(j0, valid, carry):
        heads = [slice(hh * dh, (hh + 1) * dh) for hh in range(hp)]
        zs = [lax.dot_general(q_ref[:, sl], k_ref[pl.ds(j0, tq), sl], nt, preferred_element_type=F32) * scale
              + bias_ref[hh] for hh, sl in enumerate(heads)]
        outs, runs = _sb_blocks(zs, valid, [v_ref[pl.ds(j0, tq), sl] for sl in heads], [c[1] for c in carry], tri)
        return tuple((c[0] + out, run) for c, out, run in zip(carry, outs, runs))

    zero = (jnp.zeros((tq, dh), F32), jnp.zeros((tq, 1), F32))
    carry = block(pl.multiple_of(i * tq, tq), strict, (zero,) * hp)
    carry = lax.fori_loop(1, i + 1, lambda s, c: block(pl.multiple_of((i - s) * tq, tq), None, c), carry)
    for hh in range(hp):
        o_ref[:, hh * dh:(hh + 1) * dh] = carry[hh][0].astype(o_ref.dtype)


def _sb_prompt(qb, kb, vb, b_sb, l, bsz, t, dh):
    m, w = qb.shape
    nh = w // dh
    hp = math.gcd(SB_HEADS, nh)
    tq = min(256, t)
    nq = t // tq
    return pl.pallas_call(
        functools.partial(_sb_prompt_kernel, tq=tq, dh=dh, scale=1.0 / math.sqrt(dh)),
        grid=(bsz, nh // hp, nq),
        in_specs=[pl.BlockSpec((tq, hp * dh), lambda b, h, i: (b * nq + i, h)),
                  pl.BlockSpec((t, hp * dh), lambda b, h, i: (b, h)),
                  pl.BlockSpec((t, hp * dh), lambda b, h, i: (b, h)),
                  pl.BlockSpec((None, hp, 1, 1), lambda b, h, i: (l, h, 0, 0))],
        out_specs=pl.BlockSpec((tq, hp * dh), lambda b, h, i: (b * nq + i, h)),
        out_shape=jax.ShapeDtypeStruct((m, w), BF),
        compiler_params=_cp("arbitrary", "arbitrary", "arbitrary"),
        name="sb_prompt",
    )(qb, kb, vb, b_sb.reshape(b_sb.shape[0], nh, 1, 1))


def _sb_sample_kernel(pt_ref, q_ref, kn_ref, vn_ref, bias_ref, *refs, pps, nh, tq, scale):
    k_refs = refs[:pps]
    v_refs = refs[pps:2 * pps]
    o_ref = refs[2 * pps]
    acc_ref, run_ref = refs[2 * pps + 1:]
    s = pl.program_id(1)
    q = q_ref[...]
    bias = bias_ref[...]
    r = q.shape[0]
    nt = (((1,), (1,)), ((), ()))

    @pl.when(s == 0)
    def _():
        n = kn_ref.shape[0]
        rr = lax.broadcasted_iota(jnp.int32, (r, n), 0)
        cc = lax.broadcasted_iota(jnp.int32, (r, n), 1)
        valid = jnp.logical_and(rr // tq == cc % nh, cc // nh < rr % tq)
        z = lax.dot_general(q, kn_ref[...].astype(BF), nt, preferred_element_type=F32) * scale + bias
        out, run = _sb_block(z, valid, vn_ref[...].astype(BF), jnp.zeros((r, 1), F32), _tri(n))
        acc_ref[...] = out
        run_ref[...] = run

    c = k_refs[0].shape[0]
    rr = lax.broadcasted_iota(jnp.int32, (r, c), 0)
    cc = lax.broadcasted_iota(jnp.int32, (r, c), 1)
    valid = rr // tq == cc % nh
    tri = _tri(2 * LANES)
    zs = [lax.dot_general(q, k_refs[p][...].astype(BF), nt, preferred_element_type=F32) * scale + bias
          for p in range(pps)]
    outs, runs = _sb_blocks(zs, valid, [v_refs[p][...].astype(BF) for p in range(pps)], [run_ref[...]], tri,
                            chain=True)
    acc = acc_ref[...]
    for out in outs:
        acc = acc + out
    acc_ref[...] = acc
    run_ref[...] = runs[-1]

    @pl.when(s == pl.num_programs(1) - 1)
    def _():
        o_ref[...] = acc.astype(o_ref.dtype)


def _sb_sample(q_hq, k_new, v_new, bias_rows, cache_k, cache_v, page_table, l, nh, tq):
    bsz, r, dh = q_hq.shape
    n_pages = page_table.shape[1]
    pps = math.gcd(8, n_pages)
    prow = cache_k.shape[2]
    nn = k_new.shape[1]

    def page_spec(p):
        return pl.BlockSpec((None, None, prow, dh),
                            lambda b, s, pt: (l, pt[b, n_pages - 1 - (s * pps + p)], 0, 0))

    grid_spec = pltpu.PrefetchScalarGridSpec(
        num_scalar_prefetch=1,
        grid=(bsz, n_pages // pps),
        in_specs=[pl.BlockSpec((None, r, dh), lambda b, s, pt: (b, 0, 0)),
                  pl.BlockSpec((None, nn, dh), lambda b, s, pt: (b, 0, 0)),
                  pl.BlockSpec((None, nn, dh), lambda b, s, pt: (b, 0, 0)),
                  pl.BlockSpec((r, 1), lambda b, s, pt: (0, 0))]
                 + [page_spec(p) for p in range(pps)] * 2,
        out_specs=pl.BlockSpec((None, r, dh), lambda b, s, pt: (b, 0, 0)),
        scratch_shapes=[pltpu.VMEM((r, dh), F32), pltpu.VMEM((r, 1), F32)],
    )
    return pl.pallas_call(
        functools.partial(_sb_sample_kernel, pps=pps, nh=nh, tq=tq, scale=1.0 / math.sqrt(dh)),
        grid_spec=grid_spec,
        out_shape=jax.ShapeDtypeStruct((bsz, r, dh), BF),
        compiler_params=_cp("arbitrary", "arbitrary"),
        name="sb_sample",
    )(page_table, q_hq, k_new, v_new, bias_rows, *([cache_k] * pps), *([cache_v] * pps))


CONV_PAD = 32
CONV_ROWS = 64


def _conv_kernel(*refs, tt, tv, cw, has_hist):
    if has_hist:
        a_ref, g_ref, hist_ref, w_ref, bdw_ref, gln_ref, bln_ref, y_ref, st_ref, s_ref, sh_ref, c_ref = refs
    else:
        a_ref, g_ref, w_ref, bdw_ref, gln_ref, bln_ref, y_ref, st_ref, s_ref, sh_ref, c_ref = refs
    i = pl.program_id(1)
    hrows = cw - 1
    nch = a_ref.shape[1]
    rows = CONV_PAD + tt

    @pl.when(i == 0)
    def _():
        s_ref[0:CONV_PAD, :] = hist_ref[...] if has_hist else jnp.zeros((CONV_PAD, nch), F32)
        s_ref[rows:rows + SUBLANES, :] = jnp.zeros((SUBLANES, nch), F32)

    s_ref[CONV_PAD:rows, :] = a_ref[...] * jax.nn.sigmoid(g_ref[...])

    for b in range(SUBLANES):
        sh_ref[b] = s_ref[pl.ds(b, rows), :]

    rs = min(CONV_ROWS, tt)
    groups = list(range(0, tt, rs))

    def chan(c, carry):
        c0 = pl.multiple_of(c * LANES, LANES)
        lanes = pl.ds(c0, LANES)
        accs = [jnp.broadcast_to(bdw_ref[:, lanes], (rs, LANES)) for _ in groups]
        for j in range(cw):
            off = CONV_PAD - hrows + j
            wj = jnp.broadcast_to(w_ref[j:j + 1, lanes], (rs, LANES))
            for gi, r0 in enumerate(groups):
                accs[gi] = accs[gi] + wj * sh_ref[off % SUBLANES,
                                                  pl.ds(r0 + off // SUBLANES * SUBLANES, rs), lanes]
        for gi, r0 in enumerate(groups):
            c_ref[r0:r0 + rs, lanes] = accs[gi]
        return carry

    lax.fori_loop(0, nch // LANES, chan, 0)

    x = c_ref[...]
    mu = jnp.mean(x, axis=-1, keepdims=True)
    xc = x - mu
    var = jnp.mean(xc * xc, axis=-1, keepdims=True)
    y = xc * lax.rsqrt(var + EPS) * gln_ref[...] + bln_ref[...]
    y_ref[...] = (y * jax.nn.sigmoid(y)).astype(y_ref.dtype)

    @pl.when(i == pl.num_programs(1) - 1)
    def _():
        st_ref[...] = s_ref[pl.ds(CONV_PAD + tv - hrows, hrows), :]

    s_ref[0:CONV_PAD, :] = s_ref[tt:tt + CONV_PAD, :]


def _conv_module(proj3, cb_a, hist, w_dw, b_dw, g_ln, b_ln, l, t_valid):
    bsz, t, _ = proj3.shape
    cw, nch = w_dw.shape[1:]
    tt = min(256, t)
    nblk = t // tt
    tv = tt if t_valid == t else t_valid
    assert t_valid == t or nblk == 1
    vec = lambda a: a.reshape(a.shape[0], 1, nch)
    vspec = pl.BlockSpec((None, 1, nch), lambda b, i: (l, 0, 0))
    in_specs = [pl.BlockSpec((None, tt, nch), lambda b, i: (b, i, cb_a)),
                pl.BlockSpec((None, tt, nch), lambda b, i: (b, i, cb_a + 1))]
    args = [proj3, proj3]
    if hist is not None:
        in_specs.append(pl.BlockSpec((None, CONV_PAD, nch), lambda b, i: (b, 0, 0)))
        args.append(hist)
    in_specs += [pl.BlockSpec((None, cw, nch), lambda b, i: (l, 0, 0)), vspec, vspec, vspec]
    args += [w_dw, vec(b_dw), vec(g_ln), vec(b_ln)]
    return pl.pallas_call(
        functools.partial(_conv_kernel, tt=tt, tv=tv, cw=cw, has_hist=hist is not None),
        grid=(bsz, nblk),
        in_specs=in_specs,
        out_specs=[pl.BlockSpec((None, tt, nch), lambda b, i: (b, i, 0)),
                   pl.BlockSpec((None, cw - 1, nch), lambda b, i: (b, 0, 0))],
        out_shape=[jax.ShapeDtypeStruct((bsz, t, nch), BF),
                   jax.ShapeDtypeStruct((bsz, cw - 1, nch), F32)],
        scratch_shapes=[pltpu.VMEM((CONV_PAD + max(CONV_PAD, tt) + SUBLANES, nch), F32),
                        pltpu.VMEM((SUBLANES, CONV_PAD + tt, nch), F32),
                        pltpu.VMEM((tt, nch), F32)],
        compiler_params=_cp("arbitrary", "arbitrary"),
        name="conv_module",
    )(*args)


def _dot_split(x, m01, left):
    acc = None
    for part in _split3(x):
        p = (jnp.dot(m01, part, preferred_element_type=F32) if left
             else jnp.dot(part, m01, preferred_element_type=F32))
        acc = p if acc is None else acc + p
    return acc


def _mlstm_kernel(*refs, nh, dk, dv, tv, has_state, kscale):
    if has_state:
        (q_ref, k_ref, v0_ref, v1_ref, o0_ref, o1_ref, gc_ref, gr_ref, bc_ref, br_ref, ghn_ref,
         c0_ref, n0_ref, m0_ref, y_ref, cout_ref, nout_ref, mout_ref, c_s, n_s, m_s) = refs
    else:
        (q_ref, k_ref, v0_ref, v1_ref, o0_ref, o1_ref, gc_ref, gr_ref, bc_ref, br_ref, ghn_ref,
         y_ref, cout_ref, nout_ref, mout_ref, c_s, n_s, m_s) = refs
    ci = pl.program_id(1)
    ln = q_ref.shape[0]

    @pl.when(ci == 0)
    def _():
        if has_state:
            c_s[...] = c0_ref[...]
            n_s[...] = n0_ref[...]
            m_s[...] = m0_ref[...]
        else:
            c_s[...] = jnp.zeros_like(c_s)
            n_s[...] = jnp.zeros_like(n_s)
            m_s[...] = jnp.zeros_like(m_s)

    pre_c = gc_ref[...] + bc_ref[...]
    lf_c = jax.nn.log_sigmoid(pre_c)
    pre_r = gr_ref[...] + br_ref[...]
    lf_r = jax.nn.log_sigmoid(pre_r)
    if tv < ln:
        keep_c = lax.broadcasted_iota(jnp.int32, pre_c.shape, 0) < tv
        keep_r = lax.broadcasted_iota(jnp.int32, pre_r.shape, 1) < tv
        pre_c = jnp.where(keep_c, pre_c, NEG)
        pre_r = jnp.where(keep_r, pre_r, NEG)
        lf_c = jnp.where(keep_c, lf_c, 0.0)
        lf_r = jnp.where(keep_r, lf_r, 0.0)
    row = lax.broadcasted_iota(jnp.int32, (ln, ln), 0)
    col = lax.broadcasted_iota(jnp.int32, (ln, ln), 1)
    causal = col <= row
    lower = causal.astype(BF)
    upper = (row <= col).astype(BF)
    bcum_c = _dot_split(lf_c, lower, left=True)
    bcum_r = _dot_split(lf_r, upper, left=False)

    hpb = v0_ref.shape[1] // dv
    nt = (((1,), (1,)), ((), ()))
    tn = (((0,), (0,)), ((), ()))

    hd = []
    for h in range(nh):
        g = {}
        v_ref, g["o_ref"] = (v0_ref, o0_ref) if h < hpb else (v1_ref, o1_ref)
        g["vs"] = vs = slice((h % hpb) * dv, (h % hpb + 1) * dv)
        ks = slice(h * dk, (h + 1) * dk)
        bc = bcum_c[:, nh + h:nh + h + 1]
        ic = pre_c[:, h:h + 1]
        brow = bcum_r[nh + h:nh + h + 1, :]
        irow = pre_r[h:h + 1, :]
        m = m_s[:, h:h + 1]
        g["cst"] = c_s[h]
        g["nst"] = n_s[h:h + 1, :]
        g["qf"] = q_ref[:, ks]
        g["qb"] = g["qf"].astype(BF)
        kf = k_ref[:, ks] * kscale
        g["kb"] = kf.astype(BF)
        g["vb"] = v_ref[:, vs].astype(BF)
        log_d = jnp.where(causal, bc - brow + irow, NEG)
        log_inter = bc + m
        g["m_t"] = m_t = jnp.maximum(log_inter, jnp.max(log_d, axis=-1, keepdims=True))
        g["dmat"] = jnp.exp(log_d - m_t)
        g["e_inter"] = jnp.exp(log_inter - m_t)
        b_last = bc[ln - 1:ln, :]
        g["m_new"] = m_new = jnp.maximum(b_last + m, jnp.max(b_last - brow + irow, axis=-1, keepdims=True))
        g["decay"] = jnp.exp(b_last + m - m_new)
        g["wk"] = jnp.exp(b_last - bc + ic - m_new) * kf
        hd.append(g)
    for g in hd:
        g["qk"] = lax.dot_general(g["qb"], g["kb"], nt, preferred_element_type=F32)
        g["qc"] = jnp.dot(g["qb"], g["cst"].astype(BF), preferred_element_type=F32)
        g["kv"] = lax.dot_general(g["wk"].astype(BF), g["vb"], tn, preferred_element_type=F32)
    for g in hd:
        g["s"] = s = g["qk"] * g["dmat"]
        g["sv"] = jnp.dot(s.astype(BF), g["vb"], preferred_element_type=F32)
    for h, g in enumerate(hd):
        num = g["e_inter"] * g["qc"] + g["sv"]
        den = (g["e_inter"] * jnp.sum(g["qf"] * g["nst"], axis=-1, keepdims=True)
               + jnp.sum(g["s"], axis=-1, keepdims=True))
        hm = num / jnp.maximum(jnp.abs(den), jnp.exp(-g["m_t"]))
        y = hm * lax.rsqrt(jnp.mean(hm * hm, axis=-1, keepdims=True) + EPS) * ghn_ref[h:h + 1, :]
        y_ref[:, h * dv:(h + 1) * dv] = (y * jax.nn.sigmoid(g["o_ref"][:, g["vs"]])).astype(y_ref.dtype)
        c_s[h] = g["decay"] * g["cst"] + g["kv"]
        n_s[h:h + 1, :] = g["decay"] * g["nst"] + jnp.sum(g["wk"], axis=0, keepdims=True)
        m_s[:, h:h + 1] = g["m_new"]

    @pl.when(ci == pl.num_programs(1) - 1)
    def _():
        cout_ref[...] = c_s[...]
        nout_ref[...] = n_s[...]
        mout_ref[...] = m_s[:, 0:nh]


def _mlstm(proj3, cb_q, gates_c, gates_r, bias_c, bias_r, g_hn, state, l, nh, dk, dv, t_valid):
    bsz, t, _ = proj3.shape
    ln = min(256, t)
    nc = t // ln
    tv = ln if t_valid == t else t_valid
    assert t_valid == t or nc == 1
    assert nh * dk == CB and nh * dv == 2 * CB
    blk = lambda c: pl.BlockSpec((None, ln, CB), lambda b, i: (b, i, cb_q + c))
    in_specs = [blk(0), blk(1), blk(2), blk(3), blk(4), blk(5),
                pl.BlockSpec((None, ln, LANES), lambda b, i: (b, i, 0)),
                pl.BlockSpec((None, 2 * nh, ln), lambda b, i: (b, 0, i)),
                pl.BlockSpec((None, 1, LANES), lambda b, i: (l, 0, 0)),
                pl.BlockSpec((None, 2 * nh, 1), lambda b, i: (l, 0, 0)),
                pl.BlockSpec((None, nh, dv), lambda b, i: (l, 0, 0))]
    args = [proj3] * 6 + [gates_c, gates_r, bias_c, bias_r, g_hn]
    st_specs = [pl.BlockSpec((None, nh, dk, dv), lambda b, i: (b, 0, 0, 0)),
                pl.BlockSpec((None, nh, dk), lambda b, i: (b, 0, 0)),
                pl.BlockSpec((None, 1, nh), lambda b, i: (b, 0, 0))]
    if state is not None:
        in_specs += st_specs
        args += list(state)
    return pl.pallas_call(
        functools.partial(_mlstm_kernel, nh=nh, dk=dk, dv=dv, tv=tv, has_state=state is not None,
                          kscale=1.0 / math.sqrt(dk)),
        grid=(bsz, nc),
        in_specs=in_specs,
        out_specs=[pl.BlockSpec((None, ln, nh * dv), lambda b, i: (b, i, 0))] + st_specs,
        out_shape=[jax.ShapeDtypeStruct((bsz, t, nh * dv), BF),
                   jax.ShapeDtypeStruct((bsz, nh, dk, dv), F32),
                   jax.ShapeDtypeStruct((bsz, nh, dk), F32),
                   jax.ShapeDtypeStruct((bsz, 1, nh), F32)],
        scratch_shapes=[pltpu.VMEM((nh, dk, dv), F32), pltpu.VMEM((nh, dk), F32), pltpu.VMEM((1, nh), F32)],
        compiler_params=_cp("arbitrary", "arbitrary"),
        name="mlstm",
    )(*args)


def _mixers(proj, gates_c, shape, p, l, dims, past):
    bsz, t, tv = shape
    m = bsz * t
    w_sb, nh_sb, dh = dims["w_sb"], dims["nh_sb"], dims["dh"]
    nh, dk, dv = dims["nh_ml"], dims["dk"], dims["dv"]
    proj3 = proj.reshape(bsz, t, -1)

    qb, kf, kb, vf, vb = _qkv_prep(proj, p["g_q"], p["g_k"], l, w_sb)
    if past is None:
        ya = _sb_prompt(qb, kb, vb, p["b_sb"], l, bsz, t, dh)
        k_out, v_out = kf.reshape(bsz, t, nh_sb, dh), vf.reshape(bsz, t, nh_sb, dh)
    else:
        q_hq = qb.reshape(bsz, t, nh_sb, dh)[:, :tv].transpose(0, 2, 1, 3).reshape(bsz, nh_sb * tv, dh)
        k_out = kf.reshape(bsz, t, nh_sb, dh)[:, :tv]
        v_out = vf.reshape(bsz, t, nh_sb, dh)[:, :tv]
        bias_rows = jnp.repeat(p["b_sb"][l], tv).reshape(nh_sb * tv, 1)
        o = _sb_sample(q_hq, k_out.reshape(bsz, tv * nh_sb, dh), v_out.reshape(bsz, tv * nh_sb, dh), bias_rows,
                       past["k"], past["v"], past["page_table"], l, nh_sb, tv)
        o = o.reshape(bsz, nh_sb, tv, dh).transpose(0, 2, 1, 3).reshape(bsz, tv, w_sb)
        ya = jnp.pad(o, ((0, 0), (0, t - tv), (0, 0))).reshape(m, w_sb)

    hist = None if past is None else past["conv"]
    yb, conv_out = _conv_module(proj3, dims["cb_glu"], hist, p["w_dw"], p["b_dw"], p["g_ln"], p["b_ln"], l, tv)

    gates3 = gates_c.reshape(bsz, t, LANES)
    gates_r = gates3[:, :, :2 * nh].transpose(0, 2, 1)
    state = None if past is None else (past["C"], past["n"], past["m"])
    yc, c_out, n_out, m_out = _mlstm(proj3, dims["cb_qm"], gates3, gates_r, p["gate_bias_c"], p["gate_bias_r"],
                                     p["g_hn"], state, l, nh, dk, dv, tv)
    return (ya, yb.reshape(m, -1), yc.reshape(m, -1)), (k_out, v_out, conv_out, c_out, n_out, m_out.reshape(bsz, nh))


def _layer(xp, xs, mod_p, mod_s, p, l, dims, past, shape_p, shape_s):
    t = shape_p[1]
    ms = xs.shape[0]
    tm = min(2048, t)
    tm_half = min(1024, t)
    w_in_t = p["w_in_t"]

    hp = _norm_mod(xp, p["g_norm1"], mod_p, l, 1, 0, t)
    hs = _norm_mod(xs, p["g_norm1"], mod_s, l, 1, 0, ms)
    proj_p, proj_s = _mm(hp, hs, w_in_t, l, 0, dims["n_a"], tm, 512)
    gates_p, gates_s = _mm(hp, hs, w_in_t, l, dims["n_a"] // LANES, LANES, tm, LANES)

    ys_p, st_p = _mixers(proj_p, gates_p, shape_p, p, l, dims, None)
    ys_s, st_s = _mixers(proj_s, gates_s, shape_s, p, l, dims, past)

    mg_p, mg_s = _merge((hp,) + ys_p, (hs,) + ys_s, p["wg_t"], p["w_pa"], p["w_pb"], p["w_pc"],
                        l, tm_half, 256)
    xp, xs = _mm_residual(mg_p, mg_s, p["w_out"], l, xp, xs, mod_p, mod_s, 2, t, tm, 256)

    hp = _norm_mod(xp, p["g_norm2"], mod_p, l, 4, 3, t)
    hs = _norm_mod(xs, p["g_norm2"], mod_s, l, 4, 3, ms)
    u_p, u_s = _ffn13(hp, hs, p["w_f1"], p["w_f3"], l, tm, 256)
    xp, xs = _mm_residual(u_p, u_s, p["w_f2"], l, xp, xs, mod_p, mod_s, 5, t, tm_half, 256, kc=dims["ffn_kc"])
    return xp, xs, st_p, st_s


def kernel(x_prompt, x_sample, cache_k, cache_v, state_conv, state_C, state_n, state_m, page_table,
           c_prompt, c_sample, w_ada, b_ada, g_norm1, g_norm2, w_in, g_q, g_k, b_sb, w_dw, b_dw, g_ln, b_ln,
           b_ig, b_fg, g_hnorm, w_pa, w_pb, w_pc, w_out, w_ffn1, w_ffn3, w_ffn2):
    depth, d, n_in = w_in.shape
    bsz, t, _ = x_prompt.shape
    dbs, dt, _ = x_sample.shape
    w_sb = w_pa.shape[1]
    dh = g_q.shape[1]
    nch = w_pb.shape[1]
    nh_ml, dv = g_hnorm.shape[1:]
    w_qk = (n_in - 3 * w_sb - 2 * nch - 2 * nh_ml * dv - 2 * nh_ml - 3 * d) // 2
    off_glu = 3 * w_sb
    off_qm = off_glu + 2 * nch
    off_if = off_qm + 2 * w_qk + 2 * nh_ml * dv
    off_gates = off_if + 2 * nh_ml
    assert w_sb == CB and nch == CB and w_qk == CB and off_if % 512 == 0
    ff = w_ffn1.shape[2]
    kc = next((ff // c for c in range(2, 17) if ff % (c * LANES) == 0), ff)
    dims = dict(w_sb=w_sb, nh_sb=w_sb // dh, dh=dh, nh_ml=nh_ml, dk=w_qk // nh_ml, dv=dv, n_a=off_if,
                cb_glu=off_glu // CB, cb_qm=off_qm // CB, ffn_kc=kc)

    nc = bsz + dbs
    rpad = -(-nc // 16) * 16
    c_all = jnp.pad(jnp.concatenate([c_prompt, c_sample], axis=0), ((0, rpad - nc), (0, 0))).astype(BF)
    mod = _adaln(c_all, w_ada, b_ada).reshape(depth, rpad, 6, d)
    mod_p = mod[:, :bsz].transpose(0, 2, 1, 3).reshape(depth, 6, bsz, 1, d)
    mod_s = jnp.repeat(mod[:, bsz:nc], SAMPLE_T, axis=1).transpose(0, 2, 1, 3).reshape(depth, 6, 1, dbs * SAMPLE_T, d)

    w_in_t = jnp.swapaxes(w_in, 1, 2)
    wg_t = _cast_rows(w_in_t, off_gates, 3 * d)
    gate_bias = jnp.concatenate([b_ig, b_fg], axis=1)
    params = dict(g_norm1=g_norm1, g_norm2=g_norm2, w_in_t=w_in_t, wg_t=wg_t, g_q=g_q, g_k=g_k,
                  b_sb=b_sb, w_dw=w_dw, b_dw=b_dw, g_ln=g_ln, b_ln=b_ln,
                  gate_bias_c=jnp.pad(gate_bias, ((0, 0), (0, LANES - 2 * nh_ml))).reshape(depth, 1, LANES),
                  gate_bias_r=gate_bias.reshape(depth, 2 * nh_ml, 1),
                  g_hn=g_hnorm, w_pa=w_pa, w_pb=w_pb, w_pc=w_pc, w_out=w_out,
                  w_f1=w_ffn1, w_f3=w_ffn3, w_f2=w_ffn2)

    pool, page, nh_sb = cache_k.shape[1], cache_k.shape[2], cache_k.shape[3]
    ck = cache_k.reshape(depth, pool, page * nh_sb, dh)
    cv = cache_v.reshape(depth, pool, page * nh_sb, dh)
    hist = jnp.pad(state_conv, ((0, 0), (0, 0), (CONV_PAD - state_conv.shape[2], 0), (0, 0)))

    yp = x_prompt.reshape(bsz * t, d)
    ys = jnp.pad(x_sample, ((0, 0), (0, SAMPLE_T - dt), (0, 0))).reshape(dbs * SAMPLE_T, d)
    outs_p = [[] for _ in range(6)]
    outs_s = [[] for _ in range(6)]
    for l in range(depth):
        past = dict(k=ck, v=cv, page_table=page_table, conv=hist[l], C=state_C[l], n=state_n[l],
                    m=state_m[l].reshape(dbs, 1, nh_ml))
        yp, ys, st_p, st_s = _layer(yp, ys, mod_p, mod_s, params, l, dims, past,
                                    (bsz, t, t), (dbs, SAMPLE_T, dt))
        for i in range(6):
            outs_p[i].append(st_p[i])
            outs_s[i].append(st_s[i])
    outs_p = [jnp.stack(o) for o in outs_p]
    outs_s = [jnp.stack(o) for o in outs_s]
    return (yp.reshape(bsz, t, d), ys.reshape(dbs, SAMPLE_T, d)[:, :dt], *outs_p, *outs_s)
```

```python
import functools
import math

import jax
import jax.numpy as jnp
from jax import lax
from jax.experimental import pallas as pl
from jax.experimental.pallas import tpu as pltpu

BF = jnp.bfloat16
F32 = jnp.float32
EPS = 1e-6
NEG = -1e30
LANES = 128
SUBLANES = 8
CB = 1024
SAMPLE_T = 8
VMEM_LIMIT = 60 * 1024 * 1024


def _cp(*sem):
    return pltpu.CompilerParams(dimension_semantics=sem, vmem_limit_bytes=VMEM_LIMIT)


def _resident(shape, index_map):
    return pl.BlockSpec(shape, index_map, pipeline_mode=pl.Buffered(1))


def _mod_spec(l, which, rows_per_group, tm, tn=None):
    bpg = rows_per_group // tm

    def full(i, *_):
        return (l, which, i // bpg, 0, 0)

    def tiled(i, j):
        return (l, which, i // bpg, 0, j)

    return full if tn is None else tiled


def _wdot(a, w_ref, kc=None):
    k = w_ref.shape[0]
    kc = k if kc is None else kc
    acc = None
    for k0 in range(0, k, kc):
        w = w_ref[k0:k0 + kc, :]
        if w.dtype != BF:
            w = w.astype(BF)
        p = jnp.dot(a[:, k0:k0 + kc], w, preferred_element_type=F32)
        acc = p if acc is None else acc + p
    return acc


def _wdot_t(a, wt_ref):
    return lax.dot_general(a[...], wt_ref[...].astype(BF), (((1,), (1,)), ((), ())), preferred_element_type=F32)


def _ada_kernel(c_ref, w_ref, b_ref, o_ref):
    o_ref[...] = _wdot(c_ref, w_ref) + b_ref[...]


def _adaln(c_all, w_ada, b_ada):
    nl, d, n = w_ada.shape
    r = c_all.shape[0]
    tn = 512
    return pl.pallas_call(
        _ada_kernel,
        grid=(nl, n // tn),
        in_specs=[pl.BlockSpec((r, d), lambda l, j: (0, 0)),
                  pl.BlockSpec((None, d, tn), lambda l, j: (l, 0, j)),
                  pl.BlockSpec((None, 1, tn), lambda l, j: (l, 0, j))],
        out_specs=pl.BlockSpec((None, r, tn), lambda l, j: (l, 0, j)),
        out_shape=jax.ShapeDtypeStruct((nl, r, n), F32),
        compiler_params=_cp("arbitrary", "arbitrary"),
        name="adaln",
    )(c_all, w_ada, b_ada.reshape(nl, 1, n))


def _norm_kernel(x_ref, g_ref, sc_ref, sh_ref, o_ref):
    x = x_ref[...]
    y = x * lax.rsqrt(jnp.mean(x * x, axis=-1, keepdims=True) + EPS) * g_ref[...]
    o_ref[...] = (y * (1.0 + sc_ref[...]) + sh_ref[...]).astype(o_ref.dtype)


def _norm_mod(x, g, mod5, l, which_sc, which_sh, rpg):
    m, d = x.shape
    tm = min(512, m)
    r = mod5.shape[3]
    rb = tm if r > 1 else 1
    return pl.pallas_call(
        _norm_kernel,
        grid=(m // tm,),
        in_specs=[pl.BlockSpec((tm, d), lambda i: (i, 0)),
                  pl.BlockSpec((None, 1, d), lambda i: (l, 0, 0)),
                  pl.BlockSpec((None, None, None, rb, d), _mod_spec(l, which_sc, rpg, tm)),
                  pl.BlockSpec((None, None, None, rb, d), _mod_spec(l, which_sh, rpg, tm))],
        out_specs=pl.BlockSpec((tm, d), lambda i: (i, 0)),
        out_shape=jax.ShapeDtypeStruct((m, d), BF),
        compiler_params=_cp("arbitrary"),
        name="norm_mod",
    )(x, g.reshape(g.shape[0], 1, d), mod5, mod5)


def _sample_tile(os_ref, fn):
    first = pl.program_id(0) == 0

    @pl.when(first)
    def _():
        os_ref[...] = fn().astype(os_ref.dtype)

    @pl.when(jnp.logical_not(first))
    def _():
        os_ref[...] = jnp.zeros_like(os_ref)


def _sample_out(nblk, ms, n, dtype, tn):
    return (pl.BlockSpec((None, ms, tn), lambda i, j: (i, 0, j)), jax.ShapeDtypeStruct((nblk, ms, n), dtype))


def _inproj_kernel(a_ref, wt_ref, wx_ref, as_ref, o_ref, ox_ref, os_ref, osx_ref):
    o_ref[...] = _wdot_t(a_ref, wt_ref)
    _sample_tile(os_ref, lambda: _wdot_t(as_ref, wt_ref))

    @pl.when(pl.program_id(1) == 0)
    def _():
        ox_ref[...] = _wdot_t(a_ref, wx_ref)
        _sample_tile(osx_ref, lambda: _wdot_t(as_ref, wx_ref))


def _inproj(a, a_s, w_t, l, n_cols, tm, tn):
    m, k = a.shape
    ms = a_s.shape[0]
    nblk = m // tm
    os_spec, os_shape = _sample_out(nblk, ms, n_cols, F32, tn)
    o, ox, o_s, osx = pl.pallas_call(
        _inproj_kernel,
        grid=(nblk, n_cols // tn),
        in_specs=[_resident((tm, k), lambda i, j: (i, 0)),
                  pl.BlockSpec((None, tn, k), lambda i, j: (l, j, 0)),
                  pl.BlockSpec((None, LANES, k), lambda i, j: (l, n_cols // LANES, 0)),
                  _resident((ms, k), lambda i, j: (0, 0))],
        out_specs=[pl.BlockSpec((tm, tn), lambda i, j: (i, j)),
                   pl.BlockSpec((tm, LANES), lambda i, j: (i, 0)),
                   os_spec,
                   pl.BlockSpec((None, ms, LANES), lambda i, j: (i, 0, 0))],
        out_shape=[jax.ShapeDtypeStruct((m, n_cols), F32), jax.ShapeDtypeStruct((m, LANES), F32),
                   os_shape, jax.ShapeDtypeStruct((nblk, ms, LANES), F32)],
        compiler_params=_cp("arbitrary", "arbitrary"),
        name="inproj",
    )(a, w_t, w_t, a_s)
    return o, ox, o_s[0], osx[0]


def _mm_res_kernel(a_ref, w_ref, x_ref, gt_ref, as_ref, xs_ref, gts_ref, o_ref, os_ref, *, kc):
    o_ref[...] = x_ref[...] + gt_ref[...] * _wdot(a_ref, w_ref, kc)
    _sample_tile(os_ref, lambda: xs_ref[...] + gts_ref[...] * _wdot(as_ref, w_ref, kc))


def _mm_residual(a, a_s, w, l, x, x_s, mod_p, mod_s, which_gt, rpg, tm, tn, kc=None):
    m, k = a.shape
    ms = a_s.shape[0]
    n = w.shape[2]
    os_spec, os_shape = _sample_out(m // tm, ms, n, F32, tn)
    o, o_s = pl.pallas_call(
        functools.partial(_mm_res_kernel, kc=kc),
        grid=(m // tm, n // tn),
        in_specs=[_resident((tm, k), lambda i, j: (i, 0)),
                  pl.BlockSpec((None, k, tn), lambda i, j: (l, 0, j)),
                  pl.BlockSpec((tm, tn), lambda i, j: (i, j)),
                  pl.BlockSpec((None, None, None, 1, tn), _mod_spec(l, which_gt, rpg, tm, tn)),
                  _resident((ms, k), lambda i, j: (0, 0)),
                  pl.BlockSpec((ms, tn), lambda i, j: (0, j)),
                  pl.BlockSpec((None, None, None, ms, tn), lambda i, j: (l, which_gt, 0, 0, j))],
        out_specs=[pl.BlockSpec((tm, tn), lambda i, j: (i, j)), os_spec],
        out_shape=[jax.ShapeDtypeStruct((m, n), F32), os_shape],
        compiler_params=_cp("arbitrary", "arbitrary"),
        name="mm_residual",
    )(a, w, x, mod_p, a_s, x_s, mod_s)
    return o, o_s[0]


def _merge_kernel(h_ref, ya_ref, yb_ref, yc_ref, g0_ref, g1_ref, g2_ref, wa_ref, wb_ref, wc_ref,
                  hs_ref, yas_ref, ybs_ref, ycs_ref, o_ref, os_ref):
    def merged(h_r, ya_r, yb_r, yc_r):
        h = h_r[...]
        out = jax.nn.sigmoid(_wdot_t(h, g0_ref)) * _wdot(ya_r, wa_ref)
        out = out + jax.nn.sigmoid(_wdot_t(h, g1_ref)) * _wdot(yb_r, wb_ref)
        return out + jax.nn.sigmoid(_wdot_t(h, g2_ref)) * _wdot(yc_r, wc_ref)

    o_ref[...] = merged(h_ref, ya_ref, yb_ref, yc_ref).astype(o_ref.dtype)
    _sample_tile(os_ref, lambda: merged(hs_ref, yas_ref, ybs_ref, ycs_ref))


def _cast_rows_kernel(x_ref, o_ref):
    o_ref[...] = x_ref[0].astype(o_ref.dtype)


def _cast_rows(w_t, row0, nrows):
    nl, _, k = w_t.shape
    tr = 512
    assert row0 % 16 == 0 and nrows % tr == 0
    return pl.pallas_call(
        _cast_rows_kernel,
        grid=(nl, nrows // tr),
        in_specs=[pl.BlockSpec((pl.Element(1), pl.Element(tr), pl.Element(k)),
                               lambda l, r: (l, pl.multiple_of(row0 + r * tr, 16), 0))],
        out_specs=pl.BlockSpec((None, tr, k), lambda l, r: (l, r, 0)),
        out_shape=jax.ShapeDtypeStruct((nl, nrows, k), BF),
        compiler_params=_cp("arbitrary", "arbitrary"),
        name="cast_rows",
    )(w_t)


def _merge(acts, acts_s, wg_t, w_pa, w_pb, w_pc, l, tm, tn):
    m, d = acts[0].shape
    ms = acts_s[0].shape[0]
    nb = d // tn

    def gspec(b):
        return pl.BlockSpec((None, tn, d), lambda i, j: (l, b * nb + j, 0))

    def wspec(k):
        return pl.BlockSpec((None, k, tn), lambda i, j: (l, 0, j))

    os_spec, os_shape = _sample_out(m // tm, ms, d, BF, tn)
    o, o_s = pl.pallas_call(
        _merge_kernel,
        grid=(m // tm, d // tn),
        in_specs=[_resident((tm, a.shape[1]), lambda i, j: (i, 0)) for a in acts]
                 + [gspec(0), gspec(1), gspec(2)]
                 + [wspec(a.shape[1]) for a in acts[1:]]
                 + [_resident((ms, a.shape[1]), lambda i, j: (0, 0)) for a in acts_s],
        out_specs=[pl.BlockSpec((tm, tn), lambda i, j: (i, j)), os_spec],
        out_shape=[jax.ShapeDtypeStruct((m, d), BF), os_shape],
        compiler_params=_cp("arbitrary", "arbitrary"),
        name="merge",
    )(*acts, wg_t, wg_t, wg_t, w_pa, w_pb, w_pc, *acts_s)
    return o, o_s[0]


def _ffn13_kernel(a_ref, w1_ref, w3_ref, as_ref, o_ref, os_ref):
    def swiglu(a_r):
        a = a_r[...]
        return jax.nn.silu(_wdot(a, w1_ref)) * _wdot(a, w3_ref)

    o_ref[...] = swiglu(a_ref).astype(o_ref.dtype)
    _sample_tile(os_ref, lambda: swiglu(as_ref))


def _ffn13(a, a_s, w1, w3, l, tm, tn):
    m, k = a.shape
    ms = a_s.shape[0]
    n = w1.shape[2]
    os_spec, os_shape = _sample_out(m // tm, ms, n, BF, tn)
    o, o_s = pl.pallas_call(
        _ffn13_kernel,
        grid=(m // tm, n // tn),
        in_specs=[_resident((tm, k), lambda i, j: (i, 0)),
                  pl.BlockSpec((None, k, tn), lambda i, j: (l, 0, j)),
                  pl.BlockSpec((None, k, tn), lambda i, j: (l, 0, j)),
                  _resident((ms, k), lambda i, j: (0, 0))],
        out_specs=[pl.BlockSpec((tm, tn), lambda i, j: (i, j)), os_spec],
        out_shape=[jax.ShapeDtypeStruct((m, n), BF), os_shape],
        compiler_params=_cp("arbitrary", "arbitrary"),
        name="ffn13",
    )(a, w1, w3, a_s)
    return o, o_s[0]


def _qkv_kernel(q_ref, k_ref, v_ref, gq_ref, gk_ref, qb_ref, kf_ref, kb_ref, vf_ref, vb_ref, *, nh, dh):
    for h in range(nh):
        sl = slice(h * dh, (h + 1) * dh)
        q = q_ref[:, sl]
        k = k_ref[:, sl]
        qn = q * lax.rsqrt(jnp.mean(q * q, axis=-1, keepdims=True) + EPS) * gq_ref[...]
        kn = k * lax.rsqrt(jnp.mean(k * k, axis=-1, keepdims=True) + EPS) * gk_ref[...]
        qb_ref[:, sl] = qn.astype(BF)
        kf_ref[:, sl] = kn
        kb_ref[:, sl] = kn.astype(BF)
    v = v_ref[...]
    vf_ref[...] = v
    vb_ref[...] = v.astype(BF)


def _qkv_prep(proj, g_q, g_k, l, w_sb):
    m = proj.shape[0]
    dh = g_q.shape[1]
    tm = min(256, m)
    cb = lambda c: pl.BlockSpec((tm, w_sb), lambda i: (i, c))
    gspec = pl.BlockSpec((None, 1, dh), lambda i: (l, 0, 0))
    ob = pl.BlockSpec((tm, w_sb), lambda i: (i, 0))
    sd = lambda dt: jax.ShapeDtypeStruct((m, w_sb), dt)
    return pl.pallas_call(
        functools.partial(_qkv_kernel, nh=w_sb // dh, dh=dh),
        grid=(m // tm,),
        in_specs=[cb(0), cb(1), cb(2), gspec, gspec],
        out_specs=[ob, ob, ob, ob, ob],
        out_shape=[sd(BF), sd(F32), sd(BF), sd(F32), sd(BF)],
        compiler_params=_cp("arbitrary"),
        name="qkv_prep",
    )(proj, proj, proj, g_q.reshape(-1, 1, dh), g_k.reshape(-1, 1, dh))


def _split3(x):
    x1 = x.astype(BF)
    r1 = x - x1.astype(F32)
    x2 = r1.astype(BF)
    x3 = (r1 - x2.astype(F32)).astype(BF)
    return x1, x2, x3


def _softplus(z):
    return jnp.maximum(z, 0.0) + jnp.log(1.0 + jnp.exp(-jnp.abs(z)))


def _sb_blocks(zs, valid, vbs, runs, tri, chain=False):
    r, c = zs[0].shape
    cw = tri.shape[0]
    nchunk = c // cw
    cat = lambda parts, axis: parts[0] if len(parts) == 1 else jnp.concatenate(parts, axis=axis)
    chunks = lambda x: [x[:, i * cw:(i + 1) * cw] for i in range(nchunk)]

    lks, after_sts, tots = [], [], []
    for z in zs:
        lk = -_softplus(z)
        if valid is not None:
            lk = jnp.where(valid, lk, 0.0)
        st = cat(chunks(lk), 0)
        hi = st.astype(BF)
        lo = (st - hi.astype(F32)).astype(BF)
        lks.append(lk)
        tots.append([jnp.sum(ch, axis=-1, keepdims=True) for ch in chunks(lk)])
        after_sts.append(jnp.dot(hi, tri, preferred_element_type=F32)
                         + jnp.dot(lo, tri, preferred_element_type=F32))
    ws, new_runs = [], []
    for n, (z, lk, after_st, tot) in enumerate(zip(zs, lks, after_sts, tots)):
        suffix = new_runs[-1] if (chain and n > 0) else runs[n]
        parts = [None] * nchunk
        for i in reversed(range(nchunk)):
            parts[i] = after_st[i * r:(i + 1) * r] + suffix
            suffix = suffix + tot[i]
        w = jnp.exp(z + lk + cat(parts, 1))
        if valid is not None:
            w = jnp.where(valid, w, 0.0)
        ws.append(w.astype(BF))
        new_runs.append(suffix)
    outs = [jnp.dot(w, vb, preferred_element_type=F32) for w, vb in zip(ws, vbs)]
    return outs, new_runs


def _sb_block(z, valid, vb, run, tri):
    outs, runs = _sb_blocks([z], valid, [vb], [run], tri)
    return outs[0], runs[0]


def _tri(n):
    row = lax.broadcasted_iota(jnp.int32, (n, n), 0)
    col = lax.broadcasted_iota(jnp.int32, (n, n), 1)
    return (row > col).astype(BF)


SB_HEADS = 8


def _sb_prompt_kernel(q_ref, k_ref, v_ref, bias_ref, o_ref, *, tq, dh, scale):
    i = pl.program_id(2)
    hp = q_ref.shape[1] // dh
    tri = _tri(tq)
    row = lax.broadcasted_iota(jnp.int32, (tq, tq), 0)
    col = lax.broadcasted_iota(jnp.int32, (tq, tq), 1)
    strict = col < row
    nt = (((1,), (1,)), ((), ()))

    def block(j0, valid, carry):
        heads = [slice(hh * dh, (hh + 1) * dh) for hh in range(hp)]
        zs = [lax.dot_general(q_ref[:, sl], k_ref[pl.ds(j0, tq), sl], nt, preferred_element_type=F32) * scale
              + bias_ref[hh] for hh, sl in enumerate(heads)]
        outs, runs = _sb_blocks(zs, valid, [v_ref[pl.ds(j0, tq), sl] for sl in heads], [c[1] for c in carry], tri)
        return tuple((c[0] + out, run) for c, out, run in zip(carry, outs, runs))

    zero = (jnp.zeros((tq, dh), F32), jnp.zeros((tq, 1), F32))
    carry = block(pl.multiple_of(i * tq, tq), strict, (zero,) * hp)
    carry = lax.fori_loop(1, i + 1, lambda s, c: block(pl.multiple_of((i - s) * tq, tq), None, c), carry)
    for hh in range(hp):
        o_ref[:, hh * dh:(hh + 1) * dh] = carry[hh][0].astype(o_ref.dtype)


def _sb_prompt(qb, kb, vb, b_sb, l, bsz, t, dh):
    m, w = qb.shape
    nh = w // dh
    hp = math.gcd(SB_HEADS, nh)
    tq = min(256, t)
    nq = t // tq
    return pl.pallas_call(
        functools.partial(_sb_prompt_kernel, tq=tq, dh=dh, scale=1.0 / math.sqrt(dh)),
        grid=(bsz, nh // hp, nq),
        in_specs=[pl.BlockSpec((tq, hp * dh), lambda b, h, i: (b * nq + i, h)),
                  pl.BlockSpec((t, hp * dh), lambda b, h, i: (b, h)),
                  pl.BlockSpec((t, hp * dh), lambda b, h, i: (b, h)),
                  pl.BlockSpec((None, hp, 1, 1), lambda b, h, i: (l, h, 0, 0))],
        out_specs=pl.BlockSpec((tq, hp * dh), lambda b, h, i: (b * nq + i, h)),
        out_shape=jax.ShapeDtypeStruct((m, w), BF),
        compiler_params=_cp("arbitrary", "arbitrary", "arbitrary"),
        name="sb_prompt",
    )(qb, kb, vb, b_sb.reshape(b_sb.shape[0], nh, 1, 1))


def _sb_sample_kernel(pt_ref, q_ref, kn_ref, vn_ref, bias_ref, *refs, pps, nh, tq, scale):
    k_refs = refs[:pps]
    v_refs = refs[pps:2 * pps]
    o_ref = refs[2 * pps]
    acc_ref, run_ref = refs[2 * pps + 1:]
    s = pl.program_id(1)
    q = q_ref[...]
    bias = bias_ref[...]
    r = q.shape[0]
    nt = (((1,), (1,)), ((), ()))

    @pl.when(s == 0)
    def _():
        n = kn_ref.shape[0]
        rr = lax.broadcasted_iota(jnp.int32, (r, n), 0)
        cc = lax.broadcasted_iota(jnp.int32, (r, n), 1)
        valid = jnp.logical_and(rr // tq == cc % nh, cc // nh < rr % tq)
        z = lax.dot_general(q, kn_ref[...].astype(BF), nt, preferred_element_type=F32) * scale + bias
        out, run = _sb_block(z, valid, vn_ref[...].astype(BF), jnp.zeros((r, 1), F32), _tri(n))
        acc_ref[...] = out
        run_ref[...] = run

    c = k_refs[0].shape[0]
    rr = lax.broadcasted_iota(jnp.int32, (r, c), 0)
    cc = lax.broadcasted_iota(jnp.int32, (r, c), 1)
    valid = rr // tq == cc % nh
    tri = _tri(2 * LANES)
    zs = [lax.dot_general(q, k_refs[p][...].astype(BF), nt, preferred_element_type=F32) * scale + bias
          for p in range(pps)]
    outs, runs = _sb_blocks(zs, valid, [v_refs[p][...].astype(BF) for p in range(pps)], [run_ref[...]], tri,
                            chain=True)
    acc = acc_ref[...]
    for out in outs:
        acc = acc + out
    acc_ref[...] = acc
    run_ref[...] = runs[-1]

    @pl.when(s == pl.num_programs(1) - 1)
    def _():
        o_ref[...] = acc.astype(o_ref.dtype)


def _sb_sample(q_hq, k_new, v_new, bias_rows, cache_k, cache_v, page_table, l, nh, tq):
    bsz, r, dh = q_hq.shape
    n_pages = page_table.shape[1]
    pps = math.gcd(8, n_pages)
    prow = cache_k.shape[2]
    nn = k_new.shape[1]

    def page_spec(p):
        return pl.BlockSpec((None, None, prow, dh),
                            lambda b, s, pt: (l, pt[b, n_pages - 1 - (s * pps + p)], 0, 0))

    grid_spec = pltpu.PrefetchScalarGridSpec(
        num_scalar_prefetch=1,
        grid=(bsz, n_pages // pps),
        in_specs=[pl.BlockSpec((None, r, dh), lambda b, s, pt: (b, 0, 0)),
                  pl.BlockSpec((None, nn, dh), lambda b, s, pt: (b, 0, 0)),
                  pl.BlockSpec((None, nn, dh), lambda b, s, pt: (b, 0, 0)),
                  pl.BlockSpec((r, 1), lambda b, s, pt: (0, 0))]
                 + [page_spec(p) for p in range(pps)] * 2,
        out_specs=pl.BlockSpec((None, r, dh), lambda b, s, pt: (b, 0, 0)),
        scratch_shapes=[pltpu.VMEM((r, dh), F32), pltpu.VMEM((r, 1), F32)],
    )
    return pl.pallas_call(
        functools.partial(_sb_sample_kernel, pps=pps, nh=nh, tq=tq, scale=1.0 / math.sqrt(dh)),
        grid_spec=grid_spec,
        out_shape=jax.ShapeDtypeStruct((bsz, r, dh), BF),
        compiler_params=_cp("arbitrary", "arbitrary"),
        name="sb_sample",
    )(page_table, q_hq, k_new, v_new, bias_rows, *([cache_k] * pps), *([cache_v] * pps))


CONV_PAD = 32
CONV_ROWS = 64


def _conv_kernel(*refs, tt, tv, cw, has_hist):
    if has_hist:
        a_ref, g_ref, hist_ref, w_ref, bdw_ref, gln_ref, bln_ref, y_ref, st_ref, s_ref, sh_ref, c_ref = refs
    else:
        a_ref, g_ref, w_ref, bdw_ref, gln_ref, bln_ref, y_ref, st_ref, s_ref, sh_ref, c_ref = refs
    i = pl.program_id(1)
    hrows = cw - 1
    nch = a_ref.shape[1]
    rows = CONV_PAD + tt

    @pl.when(i == 0)
    def _():
        s_ref[0:CONV_PAD, :] = hist_ref[...] if has_hist else jnp.zeros((CONV_PAD, nch), F32)
        s_ref[rows:rows + SUBLANES, :] = jnp.zeros((SUBLANES, nch), F32)

    s_ref[CONV_PAD:rows, :] = a_ref[...] * jax.nn.sigmoid(g_ref[...])

    for b in range(SUBLANES):
        sh_ref[b] = s_ref[pl.ds(b, rows), :]

    rs = min(CONV_ROWS, tt)
    groups = list(range(0, tt, rs))

    def chan(c, carry):
        c0 = pl.multiple_of(c * LANES, LANES)
        lanes = pl.ds(c0, LANES)
        accs = [jnp.broadcast_to(bdw_ref[:, lanes], (rs, LANES)) for _ in groups]
        for j in range(cw):
            off = CONV_PAD - hrows + j
            wj = jnp.broadcast_to(w_ref[j:j + 1, lanes], (rs, LANES))
            for gi, r0 in enumerate(groups):
                accs[gi] = accs[gi] + wj * sh_ref[off % SUBLANES,
                                                  pl.ds(r0 + off // SUBLANES * SUBLANES, rs), lanes]
        for gi, r0 in enumerate(groups):
            c_ref[r0:r0 + rs, lanes] = accs[gi]
        return carry

    lax.fori_loop(0, nch // LANES, chan, 0)

    x = c_ref[...]
    mu = jnp.mean(x, axis=-1, keepdims=True)
    xc = x - mu
    var = jnp.mean(xc * xc, axis=-1, keepdims=True)
    y = xc * lax.rsqrt(var + EPS) * gln_ref[...] + bln_ref[...]
    y_ref[...] = (y * jax.nn.sigmoid(y)).astype(y_ref.dtype)

    @pl.when(i == pl.num_programs(1) - 1)
    def _():
        st_ref[...] = s_ref[pl.ds(CONV_PAD + tv - hrows, hrows), :]

    s_ref[0:CONV_PAD, :] = s_ref[tt:tt + CONV_PAD, :]


def _conv_module(proj3, cb_a, hist, w_dw, b_dw, g_ln, b_ln, l, t_valid):
    bsz, t, _ = proj3.shape
    cw, nch = w_dw.shape[1:]
    tt = min(256, t)
    nblk = t // tt
    tv = tt if t_valid == t else t_valid
    assert t_valid == t or nblk == 1
    vec = lambda a: a.reshape(a.shape[0], 1, nch)
    vspec = pl.BlockSpec((None, 1, nch), lambda b, i: (l, 0, 0))
    in_specs = [pl.BlockSpec((None, tt, nch), lambda b, i: (b, i, cb_a)),
                pl.BlockSpec((None, tt, nch), lambda b, i: (b, i, cb_a + 1))]
    args = [proj3, proj3]
    if hist is not None:
        in_specs.append(pl.BlockSpec((None, CONV_PAD, nch), lambda b, i: (b, 0, 0)))
        args.append(hist)
    in_specs += [pl.BlockSpec((None, cw, nch), lambda b, i: (l, 0, 0)), vspec, vspec, vspec]
    args += [w_dw, vec(b_dw), vec(g_ln), vec(b_ln)]
    return pl.pallas_call(
        functools.partial(_conv_kernel, tt=tt, tv=tv, cw=cw, has_hist=hist is not None),
        grid=(bsz, nblk),
        in_specs=in_specs,
        out_specs=[pl.BlockSpec((None, tt, nch), lambda b, i: (b, i, 0)),
                   pl.BlockSpec((None, cw - 1, nch), lambda b, i: (b, 0, 0))],
        out_shape=[jax.ShapeDtypeStruct((bsz, t, nch), BF),
                   jax.ShapeDtypeStruct((bsz, cw - 1, nch), F32)],
        scratch_shapes=[pltpu.VMEM((CONV_PAD + max(CONV_PAD, tt) + SUBLANES, nch), F32),
                        pltpu.VMEM((SUBLANES, CONV_PAD + tt, nch), F32),
                        pltpu.VMEM((tt, nch), F32)],
        compiler_params=_cp("arbitrary", "arbitrary"),
        name="conv_module",
    )(*args)


def _dot_split(x, m01, left):
    acc = None
    for part in _split3(x):
        p = (jnp.dot(m01, part, preferred_element_type=F32) if left
             else jnp.dot(part, m01, preferred_element_type=F32))
        acc = p if acc is None else acc + p
    return acc


def _mlstm_kernel(*refs, nh, dk, dv, tv, has_state, kscale):
    if has_state:
        (q_ref, k_ref, v0_ref, v1_ref, o0_ref, o1_ref, gc_ref, gr_ref, bc_ref, br_ref, ghn_ref,
         c0_ref, n0_ref, m0_ref, y_ref, cout_ref, nout_ref, mout_ref, c_s, n_s, m_s) = refs
    else:
        (q_ref, k_ref, v0_ref, v1_ref, o0_ref, o1_ref, gc_ref, gr_ref, bc_ref, br_ref, ghn_ref,
         y_ref, cout_ref, nout_ref, mout_ref, c_s, n_s, m_s) = refs
    ci = pl.program_id(1)
    ln = q_ref.shape[0]

    @pl.when(ci == 0)
    def _():
        if has_state:
            c_s[...] = c0_ref[...]
            n_s[...] = n0_ref[...]
            m_s[...] = m0_ref[...]
        else:
            c_s[...] = jnp.zeros_like(c_s)
            n_s[...] = jnp.zeros_like(n_s)
            m_s[...] = jnp.zeros_like(m_s)

    pre_c = gc_ref[...] + bc_ref[...]
    lf_c = jax.nn.log_sigmoid(pre_c)
    pre_r = gr_ref[...] + br_ref[...]
    lf_r = jax.nn.log_sigmoid(pre_r)
    if tv < ln:
        keep_c = lax.broadcasted_iota(jnp.int32, pre_c.shape, 0) < tv
        keep_r = lax.broadcasted_iota(jnp.int32, pre_r.shape, 1) < tv
        pre_c = jnp.where(keep_c, pre_c, NEG)
        pre_r = jnp.where(keep_r, pre_r, NEG)
        lf_c = jnp.where(keep_c, lf_c, 0.0)
        lf_r = jnp.where(keep_r, lf_r, 0.0)
    row = lax.broadcasted_iota(jnp.int32, (ln, ln), 0)
    col = lax.broadcasted_iota(jnp.int32, (ln, ln), 1)
    causal = col <= row
    lower = causal.astype(BF)
    upper = (row <= col).astype(BF)
    bcum_c = _dot_split(lf_c, lower, left=True)
    bcum_r = _dot_split(lf_r, upper, left=False)

    hpb = v0_ref.shape[1] // dv
    nt = (((1,), (1,)), ((), ()))
    tn = (((0,), (0,)), ((), ()))

    hd = []
    for h in range(nh):
        g = {}
        v_ref, g["o_ref"] = (v0_ref, o0_ref) if h < hpb else (v1_ref, o1_ref)
        g["vs"] = vs = slice((h % hpb) * dv, (h % hpb + 1) * dv)
        ks = slice(h * dk, (h + 1) * dk)
        bc = bcum_c[:, nh + h:nh + h + 1]
        ic = pre_c[:, h:h + 1]
        brow = bcum_r[nh + h:nh + h + 1, :]
        irow = pre_r[h:h + 1, :]
        m = m_s[:, h:h + 1]
        g["cst"] = c_s[h]
        g["nst"] = n_s[h:h + 1, :]
        g["qf"] = q_ref[:, ks]
        g["qb"] = g["qf"].astype(BF)
        kf = k_ref[:, ks] * kscale
        g["kb"] = kf.astype(BF)
        g["vb"] = v_ref[:, vs].astype(BF)
        log_d = jnp.where(causal, bc - brow + irow, NEG)
        log_inter = bc + m
        g["m_t"] = m_t = jnp.maximum(log_inter, jnp.max(log_d, axis=-1, keepdims=True))
        g["dmat"] = jnp.exp(log_d - m_t)
        g["e_inter"] = jnp.exp(log_inter - m_t)
        b_last = bc[ln - 1:ln, :]
        g["m_new"] = m_new = jnp.maximum(b_last + m, jnp.max(b_last - brow + irow, axis=-1, keepdims=True))
        g["decay"] = jnp.exp(b_last + m - m_new)
        g["wk"] = jnp.exp(b_last - bc + ic - m_new) * kf
        hd.append(g)
    for g in hd:
        g["qk"] = lax.dot_general(g["qb"], g["kb"], nt, preferred_element_type=F32)
        g["qc"] = jnp.dot(g["qb"], g["cst"].astype(BF), preferred_element_type=F32)
        g["kv"] = lax.dot_general(g["wk"].astype(BF), g["vb"], tn, preferred_element_type=F32)
    for g in hd:
        g["s"] = s = g["qk"] * g["dmat"]
        g["sv"] = jnp.dot(s.astype(BF), g["vb"], preferred_element_type=F32)
    for h, g in enumerate(hd):
        num = g["e_inter"] * g["qc"] + g["sv"]
        den = (g["e_inter"] * jnp.sum(g["qf"] * g["nst"], axis=-1, keepdims=True)
               + jnp.sum(g["s"], axis=-1, keepdims=True))
        hm = num / jnp.maximum(jnp.abs(den), jnp.exp(-g["m_t"]))
        y = hm * lax.rsqrt(jnp.mean(hm * hm, axis=-1, keepdims=True) + EPS) * ghn_ref[h:h + 1, :]
        y_ref[:, h * dv:(h + 1) * dv] = (y * jax.nn.sigmoid(g["o_ref"][:, g["vs"]])).astype(y_ref.dtype)
        c_s[h] = g["decay"] * g["cst"] + g["kv"]
        n_s[h:h + 1, :] = g["decay"] * g["nst"] + jnp.sum(g["wk"], axis=0, keepdims=True)
        m_s[:, h:h + 1] = g["m_new"]

    @pl.when(ci == pl.num_programs(1) - 1)
    def _():
        cout_ref[...] = c_s[...]
        nout_ref[...] = n_s[...]
        mout_ref[...] = m_s[:, 0:nh]


def _mlstm(proj3, cb_q, gates_c, gates_r, bias_c, bias_r, g_hn, state, l, nh, dk, dv, t_valid):
    bsz, t, _ = proj3.shape
    ln = min(256, t)
    nc = t // ln
    tv = ln if t_valid == t else t_valid
    assert t_valid == t or nc == 1
    assert nh * dk == CB and nh * dv == 2 * CB
    blk = lambda c: pl.BlockSpec((None, ln, CB), lambda b, i: (b, i, cb_q + c))
    in_specs = [blk(0), blk(1), blk(2), blk(3), blk(4), blk(5),
                pl.BlockSpec((None, ln, LANES), lambda b, i: (b, i, 0)),
                pl.BlockSpec((None, 2 * nh, ln), lambda b, i: (b, 0, i)),
                pl.BlockSpec((None, 1, LANES), lambda b, i: (l, 0, 0)),
                pl.BlockSpec((None, 2 * nh, 1), lambda b, i: (l, 0, 0)),
                pl.BlockSpec((None, nh, dv), lambda b, i: (l, 0, 0))]
    args = [proj3] * 6 + [gates_c, gates_r, bias_c, bias_r, g_hn]
    st_specs = [pl.BlockSpec((None, nh, dk, dv), lambda b, i: (b, 0, 0, 0)),
                pl.BlockSpec((None, nh, dk), lambda b, i: (b, 0, 0)),
                pl.BlockSpec((None, 1, nh), lambda b, i: (b, 0, 0))]
    if state is not None:
        in_specs += st_specs
        args += list(state)
    return pl.pallas_call(
        functools.partial(_mlstm_kernel, nh=nh, dk=dk, dv=dv, tv=tv, has_state=state is not None,
                          kscale=1.0 / math.sqrt(dk)),
        grid=(bsz, nc),
        in_specs=in_specs,
        out_specs=[pl.BlockSpec((None, ln, nh * dv), lambda b, i: (b, i, 0))] + st_specs,
        out_shape=[jax.ShapeDtypeStruct((bsz, t, nh * dv), BF),
                   jax.ShapeDtypeStruct((bsz, nh, dk, dv), F32),
                   jax.ShapeDtypeStruct((bsz, nh, dk), F32),
                   jax.ShapeDtypeStruct((bsz, 1, nh), F32)],
        scratch_shapes=[pltpu.VMEM((nh, dk, dv), F32), pltpu.VMEM((nh, dk), F32), pltpu.VMEM((1, nh), F32)],
        compiler_params=_cp("arbitrary", "arbitrary"),
        name="mlstm",
    )(*args)


def _mixers(proj, gates_c, shape, p, l, dims, past):
    bsz, t, tv = shape
    m = bsz * t
    w_sb, nh_sb, dh = dims["w_sb"], dims["nh_sb"], dims["dh"]
    nh, dk, dv = dims["nh_ml"], dims["dk"], dims["dv"]
    proj3 = proj.reshape(bsz, t, -1)

    qb, kf, kb, vf, vb = _qkv_prep(proj, p["g_q"], p["g_k"], l, w_sb)
    if past is None:
        ya = _sb_prompt(qb, kb, vb, p["b_sb"], l, bsz, t, dh)
        k_out, v_out = kf.reshape(bsz, t, nh_sb, dh), vf.reshape(bsz, t, nh_sb, dh)
    else:
        q_hq = qb.reshape(bsz, t, nh_sb, dh)[:, :tv].transpose(0, 2, 1, 3).reshape(bsz, nh_sb * tv, dh)
        k_out = kf.reshape(bsz, t, nh_sb, dh)[:, :tv]
        v_out = vf.reshape(bsz, t, nh_sb, dh)[:, :tv]
        bias_rows = jnp.repeat(p["b_sb"][l], tv).reshape(nh_sb * tv, 1)
        o = _sb_sample(q_hq, k_out.reshape(bsz, tv * nh_sb, dh), v_out.reshape(bsz, tv * nh_sb, dh), bias_rows,
                       past["k"], past["v"], past["page_table"], l, nh_sb, tv)
        o = o.reshape(bsz, nh_sb, tv, dh).transpose(0, 2, 1, 3).reshape(bsz, tv, w_sb)
        ya = jnp.pad(o, ((0, 0), (0, t - tv), (0, 0))).reshape(m, w_sb)

    hist = None if past is None else past["conv"]
    yb, conv_out = _conv_module(proj3, dims["cb_glu"], hist, p["w_dw"], p["b_dw"], p["g_ln"], p["b_ln"], l, tv)

    gates3 = gates_c.reshape(bsz, t, LANES)
    gates_r = gates3[:, :, :2 * nh].transpose(0, 2, 1)
    state = None if past is None else (past["C"], past["n"], past["m"])
    yc, c_out, n_out, m_out = _mlstm(proj3, dims["cb_qm"], gates3, gates_r, p["gate_bias_c"], p["gate_bias_r"],
                                     p["g_hn"], state, l, nh, dk, dv, tv)
    return (ya, yb.reshape(m, -1), yc.reshape(m, -1)), (k_out, v_out, conv_out, c_out, n_out, m_out.reshape(bsz, nh))


def _layer(xp, xs, mod_p, mod_s, p, l, dims, past, shape_p, shape_s):
    t = shape_p[1]
    ms = xs.shape[0]
    tm = min(2048, t)
    tm_half = min(1024, t)
    w_in_t = p["w_in_t"]

    hp = _norm_mod(xp, p["g_norm1"], mod_p, l, 1, 0, t)
    hs = _norm_mod(xs, p["g_norm1"], mod_s, l, 1, 0, ms)
    proj_p, gates_p, proj_s, gates_s = _inproj(hp, hs, w_in_t, l, dims["n_a"], tm, 512)

    ys_p, st_p = _mixers(proj_p, gates_p, shape_p, p, l, dims, None)
    ys_s, st_s = _mixers(proj_s, gates_s, shape_s, p, l, dims, past)

    mg_p, mg_s = _merge((hp,) + ys_p, (hs,) + ys_s, p["wg_t"], p["w_pa"], p["w_pb"], p["w_pc"],
                        l, tm_half, 256)
    xp, xs = _mm_residual(mg_p, mg_s, p["w_out"], l, xp, xs, mod_p, mod_s, 2, t, tm, 256)

    hp = _norm_mod(xp, p["g_norm2"], mod_p, l, 4, 3, t)
    hs = _norm_mod(xs, p["g_norm2"], mod_s, l, 4, 3, ms)
    u_p, u_s = _ffn13(hp, hs, p["w_f1"], p["w_f3"], l, tm, 256)
    xp, xs = _mm_residual(u_p, u_s, p["w_f2"], l, xp, xs, mod_p, mod_s, 5, t, tm_half, 256, kc=dims["ffn_kc"])
    return xp, xs, st_p, st_s


def kernel(x_prompt, x_sample, cache_k, cache_v, state_conv, state_C, state_n, state_m, page_table,
           c_prompt, c_sample, w_ada, b_ada, g_norm1, g_norm2, w_in, g_q, g_k, b_sb, w_dw, b_dw, g_ln, b_ln,
           b_ig, b_fg, g_hnorm, w_pa, w_pb, w_pc, w_out, w_ffn1, w_ffn3, w_ffn2):
    depth, d, n_in = w_in.shape
    bsz, t, _ = x_prompt.shape
    dbs, dt, _ = x_sample.shape
    w_sb = w_pa.shape[1]
    dh = g_q.shape[1]
    nch = w_pb.shape[1]
    nh_ml, dv = g_hnorm.shape[1:]
    w_qk = (n_in - 3 * w_sb - 2 * nch - 2 * nh_ml * dv - 2 * nh_ml - 3 * d) // 2
    off_glu = 3 * w_sb
    off_qm = off_glu + 2 * nch
    off_if = off_qm + 2 * w_qk + 2 * nh_ml * dv
    off_gates = off_if + 2 * nh_ml
    assert w_sb == CB and nch == CB and w_qk == CB and off_if % 512 == 0
    ff = w_ffn1.shape[2]
    kc = next((ff // c for c in range(2, 17) if ff % (c * LANES) == 0), ff)
    dims = dict(w_sb=w_sb, nh_sb=w_sb // dh, dh=dh, nh_ml=nh_ml, dk=w_qk // nh_ml, dv=dv, n_a=off_if,
                cb_glu=off_glu // CB, cb_qm=off_qm // CB, ffn_kc=kc)

    nc = bsz + dbs
    rpad = -(-nc // 16) * 16
    c_all = jnp.pad(jnp.concatenate([c_prompt, c_sample], axis=0), ((0, rpad - nc), (0, 0))).astype(BF)
    mod = _adaln(c_all, w_ada, b_ada).reshape(depth, rpad, 6, d)
    mod_p = mod[:, :bsz].transpose(0, 2, 1, 3).reshape(depth, 6, bsz, 1, d)
    mod_s = jnp.repeat(mod[:, bsz:nc], SAMPLE_T, axis=1).transpose(0, 2, 1, 3).reshape(depth, 6, 1, dbs * SAMPLE_T, d)

    w_in_t = jnp.swapaxes(w_in, 1, 2)
    wg_t = _cast_rows(w_in_t, off_gates, 3 * d)
    gate_bias = jnp.concatenate([b_ig, b_fg], axis=1)
    params = dict(g_norm1=g_norm1, g_norm2=g_norm2, w_in_t=w_in_t, wg_t=wg_t, g_q=g_q, g_k=g_k,
                  b_sb=b_sb, w_dw=w_dw, b_dw=b_dw, g_ln=g_ln, b_ln=b_ln,
                  gate_bias_c=jnp.pad(gate_bias, ((0, 0), (0, LANES - 2 * nh_ml))).reshape(depth, 1, LANES),
                  gate_bias_r=gate_bias.reshape(depth, 2 * nh_ml, 1),
                  g_hn=g_hnorm, w_pa=w_pa, w_pb=w_pb, w_pc=w_pc, w_out=w_out,
                  w_f1=w_ffn1, w_f3=w_ffn3, w_f2=w_ffn2)

    pool, page, nh_sb = cache_k.shape[1], cache_k.shape[2], cache_k.shape[3]
    ck = cache_k.reshape(depth, pool, page * nh_sb, dh)
    cv = cache_v.reshape(depth, pool, page * nh_sb, dh)
    hist = jnp.pad(state_conv, ((0, 0), (0, 0), (CONV_PAD - state_conv.shape[2], 0), (0, 0)))

    yp = x_prompt.reshape(bsz * t, d)
    ys = jnp.pad(x_sample, ((0, 0), (0, SAMPLE_T - dt), (0, 0))).reshape(dbs * SAMPLE_T, d)
    outs_p = [[] for _ in range(6)]
    outs_s = [[] for _ in range(6)]
    for l in range(depth):
        past = dict(k=ck, v=cv, page_table=page_table, conv=hist[l], C=state_C[l], n=state_n[l],
                    m=state_m[l].reshape(dbs, 1, nh_ml))
        yp, ys, st_p, st_s = _layer(yp, ys, mod_p, mod_s, params, l, dims, past,
                                    (bsz, t, t), (dbs, SAMPLE_T, dt))
        for i in range(6):
            outs_p[i].append(st_p[i])
            outs_s[i].append(st_s[i])
    outs_p = [jnp.stack(o) for o in outs_p]
    outs_s = [jnp.stack(o) for o in outs_s]
    return (yp.reshape(bsz, t, d), ys.reshape(dbs, SAMPLE_T, d)[:, :dt], *outs_p, *outs_s)
```
